```python
import jax, jax.numpy as jnp
from jax import lax
import numpy as np

D_MODEL = 2048
BATCH = 4
SEQ = 4096
DEPTH = 4

N_MIXERS = 2
GRID_W = 64
N_META = 16
NORM_EPS = 1e-6

ATTN_HEADS = 16
ATTN_KV_HEADS = 8
ATTN_GROUP = ATTN_HEADS // ATTN_KV_HEADS
ATTN_HEAD_DIM = 128
ATTN_WIDTH = ATTN_HEADS * ATTN_HEAD_DIM
ATTN_KV_WIDTH = ATTN_KV_HEADS * ATTN_HEAD_DIM
ATTN_IN = 2 * ATTN_WIDTH + 2 * ATTN_KV_WIDTH
Q_BLOCK = 128
ROPE_THETA = 10000.0
ROPE_AXIS_DIM = ATTN_HEAD_DIM // 2

GLA_HEADS = 4
GLA_KEY_DIM = D_MODEL // 2
GLA_VALUE_DIM = D_MODEL
GLA_HEAD_K = GLA_KEY_DIM // GLA_HEADS
GLA_HEAD_V = GLA_VALUE_DIM // GLA_HEADS
GLA_GATE_RANK = 16
GLA_GATE_NORMALIZER = 16.0
GLA_CHUNK = 64
GLA_IN = 2 * GLA_KEY_DIM + 2 * GLA_VALUE_DIM + 2 * GLA_GATE_RANK

N_ATTN_LAYERS = (DEPTH + 1) // 2
N_GLA_LAYERS = DEPTH // 2

kernel_name = "hybrid_attn_gla_interleaved_encoder"


def rmsnorm(x, w):
    xf = x.astype(jnp.float32)
    y = xf * lax.rsqrt(jnp.mean(xf * xf, axis=-1, keepdims=True) + NORM_EPS)
    return (y * w.astype(jnp.float32)).astype(x.dtype)


def axial_rope_angles(seq_len):
    rows = seq_len // GRID_W
    row = jnp.repeat(jnp.arange(rows), GRID_W).astype(jnp.float32)
    col = jnp.tile(jnp.arange(GRID_W), rows).astype(jnp.float32)
    inv_freq = ROPE_THETA ** (-jnp.arange(0, ROPE_AXIS_DIM, 2, dtype=jnp.float32) / ROPE_AXIS_DIM)
    meta = jnp.zeros((N_META, ROPE_AXIS_DIM // 2), jnp.float32)
    ang_row = jnp.concatenate([meta, row[:, None] * inv_freq[None]], axis=0)
    ang_col = jnp.concatenate([meta, col[:, None] * inv_freq[None]], axis=0)
    return ang_row, ang_col


def _rotate(x, ang):
    cos = jnp.cos(ang)[None, :, None, :].astype(x.dtype)
    sin = jnp.sin(ang)[None, :, None, :].astype(x.dtype)
    x1, x2 = jnp.split(x, 2, axis=-1)
    return jnp.concatenate([x1 * cos - x2 * sin, x2 * cos + x1 * sin], axis=-1)


def apply_axial_rope(x, ang_row, ang_col):
    return jnp.concatenate([_rotate(x[..., :ROPE_AXIS_DIM], ang_row),
                            _rotate(x[..., ROPE_AXIS_DIM:], ang_col)], axis=-1)


def _attend(qb, k, v):
    s = jnp.einsum('bqkgd,bskd->bkgqs', qb, k) * (ATTN_HEAD_DIM ** -0.5)
    p = jax.nn.softmax(s.astype(jnp.float32), axis=-1).astype(v.dtype)
    return jnp.einsum('bkgqs,bskd->bqkgd', p, v)


def attention_mixer(h, w_in, q_norm, k_norm, w_out, ang_row, ang_col):
    B, L, _ = h.shape
    proj = h @ w_in
    q, k, v, gate = jnp.split(
        proj, [ATTN_WIDTH, ATTN_WIDTH + ATTN_KV_WIDTH, ATTN_WIDTH + 2 * ATTN_KV_WIDTH], axis=-1)
    q = q.reshape(B, L, ATTN_HEADS, ATTN_HEAD_DIM)
    k = k.reshape(B, L, ATTN_KV_HEADS, ATTN_HEAD_DIM)
    v = v.reshape(B, L, ATTN_KV_HEADS, ATTN_HEAD_DIM)
    q = apply_axial_rope(rmsnorm(q, q_norm), ang_row, ang_col)
    k = apply_axial_rope(rmsnorm(k, k_norm), ang_row, ang_col)
    q = q.reshape(B, L, ATTN_KV_HEADS, ATTN_GROUP, ATTN_HEAD_DIM)
    o_meta = _attend(q[:, :N_META], k, v)
    n_real = L - N_META
    nb = n_real // Q_BLOCK
    q_blocks = q[:, N_META:].reshape(B, nb, Q_BLOCK, ATTN_KV_HEADS, ATTN_GROUP, ATTN_HEAD_DIM)
    o_real = lax.map(lambda qb: _attend(qb, k, v), jnp.swapaxes(q_blocks, 0, 1))
    o_real = jnp.swapaxes(o_real, 0, 1).reshape(B, n_real, ATTN_KV_HEADS, ATTN_GROUP, ATTN_HEAD_DIM)
    o = jnp.concatenate([o_meta, o_real], axis=1).reshape(B, L, ATTN_WIDTH)
    return (o * jax.nn.silu(gate)) @ w_out


def gla_chunked(q, k, v, g, strict):
    B, H, T, dk = q.shape
    dv = v.shape[-1]
    n = T // GLA_CHUNK

    def chunks(t):
        return t.reshape(B, H, n, GLA_CHUNK, t.shape[-1]).astype(jnp.float32)

    qc, kc, vc, gc = chunks(q), chunks(k), chunks(v), chunks(g)
    b = jnp.cumsum(gc, axis=3)
    b_last = b[:, :, :, -1:, :]
    q_dec = qc * jnp.exp(b)
    k_inv = kc * jnp.exp(-b)
    k_end = kc * jnp.exp(b_last - b)
    scores = jnp.einsum('bhncd,bhnsd->bhncs', q_dec, k_inv)
    mask = jnp.tril(jnp.ones((GLA_CHUNK, GLA_CHUNK), bool), k=-1 if strict else 0)
    o_intra = jnp.einsum('bhncs,bhnse->bhnce', jnp.where(mask, scores, 0.0), vc)

    def step(S, xs):
        q_t, k_t, v_t, dl = xs
        o = jnp.einsum('bhcd,bhde->bhce', q_t, S)
        S = S * dl[..., None] + jnp.einsum('bhcd,bhce->bhde', k_t, v_t)
        return S, o

    xs = (jnp.moveaxis(q_dec, 2, 0), jnp.moveaxis(k_end, 2, 0), jnp.moveaxis(vc, 2, 0),
          jnp.moveaxis(jnp.exp(b_last[:, :, :, 0, :]), 2, 0))
    S0 = jnp.zeros((B, H, dk, dv), jnp.float32)
    _, o_inter = lax.scan(step, S0, xs)
    o = o_intra + jnp.moveaxis(o_inter, 0, 2)
    return o.reshape(B, H, T, dv).astype(q.dtype)


def gla_mixer(h, w_in, gk_up, gk_bias, o_norm, w_out):
    B, L, _ = h.shape
    pad = GLA_CHUNK - N_META
    hp = jnp.pad(h, ((0, 0), (pad, 0), (0, 0)))
    T = L + pad
    proj = hp @ w_in
    s1 = GLA_KEY_DIM
    s2 = s1 + GLA_KEY_DIM
    s3 = s2 + GLA_VALUE_DIM
    s4 = s3 + GLA_VALUE_DIM
    s5 = s4 + GLA_GATE_RANK
    q, k, v, g_out, lr_f, lr_b = jnp.split(proj, [s1, s2, s3, s4, s5], axis=-1)

    def heads(t, d):
        return t.reshape(B, T, GLA_HEADS, d).transpose(0, 2, 1, 3)

    def log_gate(lr, up, bias):
        z = (lr @ up + bias).astype(jnp.float32)
        return heads(jax.nn.log_sigmoid(z) / GLA_GATE_NORMALIZER, GLA_HEAD_K)

    q = heads(q, GLA_HEAD_K) * (GLA_HEAD_K ** -0.5)
    k = heads(k, GLA_HEAD_K)
    v = heads(v, GLA_HEAD_V)
    flip = lambda t: jnp.flip(t, axis=2)
    o_f = gla_chunked(q, k, v, log_gate(lr_f, gk_up[0], gk_bias[0]), strict=False)
    o_b = flip(gla_chunked(flip(q), flip(k), flip(v),
                           flip(log_gate(lr_b, gk_up[1], gk_bias[1])), strict=True))
    o = (o_f + o_b).transpose(0, 2, 1, 3)[:, pad:]
    o = rmsnorm(o, o_norm).reshape(B, L, GLA_VALUE_DIM)
    return (o * jax.nn.silu(g_out[:, pad:])) @ w_out


def setup_inputs(seed: int = 0) -> dict:
    key = jax.random.key(seed)
    ks = jax.random.split(key, 13)
    f32 = jnp.float32
    nrm = lambda k, shape, scale: jax.random.normal(k, shape, f32) * scale
    return {
        "x": nrm(ks[0], (BATCH, SEQ, D_MODEL), 1.0),
        "meta_tokens": nrm(ks[1], (N_META, D_MODEL), 1.0),
        "pre_norm": 1.0 + nrm(ks[2], (DEPTH, D_MODEL), 0.02),
        "post_norm": 1.0 + nrm(ks[3], (DEPTH, D_MODEL), 0.02),
        "attn_w_in": nrm(ks[4], (N_ATTN_LAYERS, D_MODEL, ATTN_IN), D_MODEL ** -0.5),
        "attn_q_norm": 1.0 + nrm(ks[5], (N_ATTN_LAYERS, ATTN_HEAD_DIM), 0.02),
        "attn_k_norm": 1.0 + nrm(ks[6], (N_ATTN_LAYERS, ATTN_HEAD_DIM), 0.02),
        "attn_w_out": nrm(ks[7], (N_ATTN_LAYERS, ATTN_WIDTH, D_MODEL), ATTN_WIDTH ** -0.5),
        "gla_w_in": nrm(ks[8], (N_GLA_LAYERS, D_MODEL, GLA_IN), D_MODEL ** -0.5),
        "gla_gk_up": nrm(ks[9], (N_GLA_LAYERS, 2, GLA_GATE_RANK, GLA_KEY_DIM), GLA_GATE_RANK ** -0.5),
        "gla_gk_bias": nrm(ks[10], (N_GLA_LAYERS, 2, GLA_KEY_DIM), 0.1),
        "gla_o_norm": 1.0 + nrm(ks[11], (N_GLA_LAYERS, GLA_HEAD_V), 0.02),
        "gla_w_out": nrm(ks[12], (N_GLA_LAYERS, GLA_VALUE_DIM, D_MODEL), GLA_VALUE_DIM ** -0.5),
    }


def reference(x, meta_tokens, pre_norm, post_norm, attn_w_in, attn_q_norm, attn_k_norm,
              attn_w_out, gla_w_in, gla_gk_up, gla_gk_bias, gla_o_norm, gla_w_out):
    B, S, D = x.shape
    meta = jnp.broadcast_to(meta_tokens.astype(x.dtype)[None], (B, N_META, D))
    h = jnp.concatenate([meta, x], axis=1)
    ang_row, ang_col = axial_rope_angles(S)
    for i in range(DEPTH):
        y = rmsnorm(h, pre_norm[i])
        j = i // N_MIXERS
        if i % N_MIXERS == 0:
            y = attention_mixer(y, attn_w_in[j], attn_q_norm[j], attn_k_norm[j], attn_w_out[j],
                                ang_row, ang_col)
        else:
            y = gla_mixer(y, gla_w_in[j], gla_gk_up[j], gla_gk_bias[j], gla_o_norm[j], gla_w_out[j])
        h = h + rmsnorm(y, post_norm[i])
    return h[:, N_META:]
```

```python
import functools

import jax
import jax.numpy as jnp
from jax import lax
from jax.experimental import pallas as pl
from jax.experimental.pallas import tpu as pltpu

F32 = jnp.float32
BF16 = jnp.bfloat16

D_MODEL = 2048
N_META = 16
GRID_W = 64
NORM_EPS = 1e-6
ROPE_THETA = 10000.0

HEAD_DIM = 128
ATTN_HEADS = 16
ATTN_KV_HEADS = 8
ATTN_GROUP = ATTN_HEADS // ATTN_KV_HEADS
ATTN_WIDTH = ATTN_HEADS * HEAD_DIM
ATTN_KV_WIDTH = ATTN_KV_HEADS * HEAD_DIM
ATTN_IN = 2 * ATTN_WIDTH + 2 * ATTN_KV_WIDTH

GLA_HEADS = 4
GLA_KEY_DIM = D_MODEL // 2
GLA_VALUE_DIM = D_MODEL
GLA_HEAD_K = GLA_KEY_DIM // GLA_HEADS
GLA_HEAD_V = GLA_VALUE_DIM // GLA_HEADS
GLA_GATE_RANK = 16
GLA_GATE_NORMALIZER = 16.0
GLA_CHUNK = 64
GLA_MAIN = 2 * GLA_KEY_DIM + 2 * GLA_VALUE_DIM

LANES = 128
ROW_PAD = 128
GLA_BLOCK = 128
VMEM_LIMIT = 56 * 1024 * 1024

PROJ_TN = 1024
KV_TILE = 512


def _pick_tile(n, candidates):
    for c in candidates:
        if n % c == 0:
            return c
    raise ValueError(f"no tile in {candidates} divides {n}")


def _split_hi_lo(a):
    hi = a.astype(BF16)
    lo = (a - hi.astype(F32)).astype(BF16)
    return hi, lo


def _inproj_kernel(n_rope_tiles, has_lr, tn, *refs):
    it = iter(refs)
    h_ref, pw_ref, w_ref = next(it), next(it), next(it)
    wlr_ref = next(it) if has_lr else None
    if n_rope_tiles:
        hn_ref, c_ref, sa_ref, sb_ref = next(it), next(it), next(it), next(it)
    o_ref = next(it)
    lr_ref = next(it) if has_lr else None
    xn_sc = next(it)

    j = pl.program_id(1)

    @pl.when(j == 0)
    def _():
        x = h_ref[...]
        ms = jnp.mean(x * x, axis=-1, keepdims=True)
        xn = (x * lax.rsqrt(ms + NORM_EPS) * pw_ref[...]).astype(BF16)
        xn_sc[...] = xn
        if has_lr:
            lr_ref[...] = jnp.dot(xn, wlr_ref[...], preferred_element_type=F32)

    acc = jnp.dot(xn_sc[...], w_ref[...], preferred_element_type=F32)

    if n_rope_tiles:
        @pl.when(j < n_rope_tiles)
        def _():
            c, sa, sb = c_ref[...], sa_ref[...], sb_ref[...]
            for hh in range(tn // HEAD_DIM):
                cols = slice(hh * HEAD_DIM, (hh + 1) * HEAD_DIM)
                a = acc[:, cols]
                r = lax.rsqrt(jnp.mean(a * a, axis=-1, keepdims=True) + NORM_EPS)
                y = a * r * hn_ref[:, cols]
                y = (y * c + pltpu.roll(y, HEAD_DIM - 32, 1) * sa + pltpu.roll(y, 32, 1) * sb)
                o_ref[:, cols] = y.astype(BF16)

        @pl.when(j >= n_rope_tiles)
        def _():
            o_ref[...] = acc.astype(BF16)
    else:
        o_ref[...] = acc.astype(BF16)


def _inproj(h, pre_w, w, *, lp, w_lr=None, rope=None):
    np_, d = h.shape
    n = w.shape[1]
    tm = _pick_tile(lp, (704, 384, 128))
    tn = PROJ_TN
    tiles_per_batch = lp // tm
    has_lr = w_lr is not None
    n_rope_tiles = 0
    in_specs = [
        pl.BlockSpec((tm, d), lambda i, j: (i, 0)),
        pl.BlockSpec((1, d), lambda i, j: (0, 0)),
        pl.BlockSpec((d, tn), lambda i, j: (0, j)),
    ]
    args = [h, pre_w.reshape(1, d), w]
    if has_lr:
        in_specs.append(pl.BlockSpec((d, LANES), lambda i, j: (0, 0)))
        args.append(w_lr)
    if rope is not None:
        head_w, cos_t, sa_t, sb_t = rope
        n_rope_tiles = head_w.shape[1] // tn
        last = n_rope_tiles - 1
        in_specs.append(pl.BlockSpec((1, tn), lambda i, j: (0, jnp.minimum(j, last))))
        tab = pl.BlockSpec((tm, HEAD_DIM), lambda i, j: (i % tiles_per_batch, 0))
        in_specs += [tab, tab, tab]
        args += [head_w, cos_t, sa_t, sb_t]
    out_shape = [jax.ShapeDtypeStruct((np_, n), BF16)]
    out_specs = [pl.BlockSpec((tm, tn), lambda i, j: (i, j))]
    if has_lr:
        out_shape.append(jax.ShapeDtypeStruct((np_, LANES), F32))
        out_specs.append(pl.BlockSpec((tm, LANES), lambda i, j: (i, 0)))
    res = pl.pallas_call(
        functools.partial(_inproj_kernel, n_rope_tiles, has_lr, tn),
        grid=(np_ // tm, n // tn),
        in_specs=in_specs,
        out_specs=out_specs,
        out_shape=out_shape,
        scratch_shapes=[pltpu.VMEM((tm, d), BF16)],
        compiler_params=pltpu.CompilerParams(
            dimension_semantics=("parallel", "arbitrary"), vmem_limit_bytes=VMEM_LIMIT),
        name="inproj_rope" if rope is not None else "inproj_gla",
    )(*args)
    return res if has_lr else res[0]


def _attn_kernel(s_real, tq, q_ref, k_ref, v_ref, o_ref, m_sc, l_sc, acc_sc):
    q = jnp.concatenate([q_ref[:, g * HEAD_DIM:(g + 1) * HEAD_DIM] for g in range(ATTN_GROUP)],
                        axis=0)
    m_sc[...] = jnp.full_like(m_sc, -jnp.inf)
    l_sc[...] = jnp.zeros_like(l_sc)
    acc_sc[...] = jnp.zeros_like(acc_sc)

    def step(k, v, n_valid):
        s = lax.dot_general(q, k, (((1,), (1,)), ((), ())), preferred_element_type=F32)
        if n_valid is not None:
            col = lax.broadcasted_iota(jnp.int32, s.shape, 1)
            s = jnp.where(col < n_valid, s, -jnp.inf)
        m_prev = m_sc[...]
        m_new = jnp.maximum(m_prev, jnp.max(s, axis=-1, keepdims=True))
        alpha = jnp.exp(m_prev - m_new)
        p = jnp.exp(s - m_new)
        l_sc[...] = alpha * l_sc[...] + jnp.sum(p, axis=-1, keepdims=True)
        acc_sc[...] = alpha * acc_sc[...] + jnp.dot(p.astype(BF16), v, preferred_element_type=F32)
        m_sc[...] = m_new

    def body(c, carry):
        r0 = pl.multiple_of(c * KV_TILE, KV_TILE)
        step(k_ref[pl.ds(r0, KV_TILE), :], v_ref[pl.ds(r0, KV_TILE), :], None)
        return carry

    lax.fori_loop(0, s_real // KV_TILE, body, 0)
    step(k_ref[s_real:s_real + ROW_PAD, :], v_ref[s_real:s_real + ROW_PAD, :], N_META)

    o = acc_sc[...] / l_sc[...]
    for g in range(ATTN_GROUP):
        o_ref[:, g * HEAD_DIM:(g + 1) * HEAD_DIM] = o[g * tq:(g + 1) * tq].astype(BF16)


def _attention(proj, *, batch, lp):
    np_ = proj.shape[0]
    s_real = lp - ROW_PAD
    tq = _pick_tile(lp, (384, 128))
    nq = lp // tq
    gw = ATTN_GROUP * HEAD_DIM
    k_col0 = ATTN_WIDTH // HEAD_DIM
    v_col0 = (ATTN_WIDTH + ATTN_KV_WIDTH) // HEAD_DIM
    return pl.pallas_call(
        functools.partial(_attn_kernel, s_real, tq),
        grid=(batch, ATTN_KV_HEADS, nq),
        in_specs=[
            pl.BlockSpec((tq, gw), lambda b, h, i: (b * nq + i, h)),
            pl.BlockSpec((lp, HEAD_DIM), lambda b, h, i: (b, k_col0 + h)),
            pl.BlockSpec((lp, HEAD_DIM), lambda b, h, i: (b, v_col0 + h)),
        ],
        out_specs=pl.BlockSpec((tq, gw), lambda b, h, i: (b * nq + i, h)),
        out_shape=jax.ShapeDtypeStruct((np_, ATTN_WIDTH), BF16),
        scratch_shapes=[
            pltpu.VMEM((ATTN_GROUP * tq, 1), F32),
            pltpu.VMEM((ATTN_GROUP * tq, 1), F32),
            pltpu.VMEM((ATTN_GROUP * tq, HEAD_DIM), F32),
        ],
        compiler_params=pltpu.CompilerParams(
            dimension_semantics=("parallel", "parallel", "parallel"), vmem_limit_bytes=VMEM_LIMIT),
        name="attention",
    )(proj, proj, proj)


def _outproj_kernel(has_gate, *refs):
    it = iter(refs)
    x_ref = next(it)
    g_ref = next(it) if has_gate else None
    w_ref, h_ref, pw_ref, out_ref = next(it), next(it), next(it), next(it)
    x = x_ref[...]
    if has_gate:
        g = g_ref[...].astype(F32)
        x = (x.astype(F32) * (g * jax.nn.sigmoid(g))).astype(BF16)
    y = jnp.dot(x, w_ref[...], preferred_element_type=F32)
    r = lax.rsqrt(jnp.mean(y * y, axis=-1, keepdims=True) + NORM_EPS)
    out_ref[...] = h_ref[...] + y * r * pw_ref[...]


def _outproj(x, w, h, post_w, *, lp, gate_src=None, gate_col_block=None):
    np_, kdim = x.shape
    d = w.shape[1]
    tm = _pick_tile(lp, (384, 128))
    has_gate = gate_src is not None
    in_specs = [pl.BlockSpec((tm, kdim), lambda i: (i, 0))]
    args = [x]
    if has_gate:
        in_specs.append(pl.BlockSpec((tm, kdim), lambda i: (i, gate_col_block)))
        args.append(gate_src)
    in_specs += [
        pl.BlockSpec((kdim, d), lambda i: (0, 0)),
        pl.BlockSpec((tm, d), lambda i: (i, 0)),
        pl.BlockSpec((1, d), lambda i: (0, 0)),
    ]
    args += [w, h, post_w.reshape(1, d)]
    return pl.pallas_call(
        functools.partial(_outproj_kernel, has_gate),
        grid=(np_ // tm,),
        in_specs=in_specs,
        out_specs=pl.BlockSpec((tm, d), lambda i: (i, 0)),
        out_shape=jax.ShapeDtypeStruct((np_, d), F32),
        compiler_params=pltpu.CompilerParams(
            dimension_semantics=("parallel",), vmem_limit_bytes=VMEM_LIMIT),
        name="outproj_gated" if has_gate else "outproj",
    )(*args)


def _gla_kernel(backward, n_blocks, *refs):
    it = iter(refs)
    q_ref, k_ref, v_ref, lr_ref, up_ref, bias_ref = (next(it) for _ in range(6))
    if backward:
        of_ref, gate_ref, onorm_ref = next(it), next(it), next(it)
    out_ref, st_sc = next(it), next(it)

    i = pl.program_id(1)

    @pl.when(i == 0)
    def _():
        st_sc[...] = jnp.zeros_like(st_sc)

    is_meta = (i == n_blocks - 1) if backward else (i == 0)
    n_valid = jnp.where(is_meta, N_META, GLA_BLOCK)

    C = GLA_CHUNK
    row = lax.broadcasted_iota(jnp.int32, (C, C), 0)
    col = lax.broadcasted_iota(jnp.int32, (C, C), 1)
    if backward:
        cum_mat = (col >= row).astype(BF16)
        pair_mask = col > row
    else:
        cum_mat = (col <= row).astype(BF16)
        pair_mask = col <= row
    edge = 0 if backward else C - 1
    scale = GLA_HEAD_K ** -0.5
    gate_cols = slice(GLA_GATE_RANK, 2 * GLA_GATE_RANK) if backward else slice(0, GLA_GATE_RANK)

    for sub in ((1, 0) if backward else (0, 1)):
        r0 = sub * C
        rows = slice(r0, r0 + C)
        valid = (lax.broadcasted_iota(jnp.int32, (C, 1), 0) + r0) < n_valid
        lr_hi, lr_lo = _split_hi_lo(lr_ref[rows, gate_cols])
        for hd in range(GLA_HEADS):
            kc = slice(hd * GLA_HEAD_K, (hd + 1) * GLA_HEAD_K)
            vc = slice(hd * GLA_HEAD_V, (hd + 1) * GLA_HEAD_V)
            up_hi, up_lo = _split_hi_lo(up_ref[:, kc])
            z = (jnp.dot(lr_hi, up_hi, preferred_element_type=F32)
                 + jnp.dot(lr_hi, up_lo, preferred_element_type=F32)
                 + jnp.dot(lr_lo, up_hi, preferred_element_type=F32)) + bias_ref[:, kc]
            g = jnp.where(valid, jax.nn.log_sigmoid(z) / GLA_GATE_NORMALIZER, 0.0)
            g_hi, g_lo = _split_hi_lo(g)
            b = (jnp.dot(cum_mat, g_hi, preferred_element_type=F32)
                 + jnp.dot(cum_mat, g_lo, preferred_element_type=F32))
            b_edge = b[edge:edge + 1, :]
            q = q_ref[rows, kc].astype(F32)
            k = jnp.where(valid, k_ref[rows, kc].astype(F32), 0.0)
            v = jnp.where(valid, v_ref[rows, vc], jnp.zeros((), BF16))
            q_dec = (q * jnp.exp(b) * scale).astype(BF16)
            k_inv = (k * jnp.exp(-b)).astype(BF16)
            k_end = (k * jnp.exp(b_edge - b)).astype(BF16)
            decay = jnp.exp(b_edge)
            sc = lax.dot_general(q_dec, k_inv, (((1,), (1,)), ((), ())),
                                 preferred_element_type=F32)
            sc = jnp.where(pair_mask, sc, 0.0).astype(BF16)
            st = st_sc[hd]
            o = (jnp.dot(sc, v, preferred_element_type=F32)
                 + lax.dot_general(q_dec, st.astype(BF16), (((1,), (1,)), ((), ())),
                                   preferred_element_type=F32))
            st_sc[hd] = st * decay + lax.dot_general(
                v, k_end, (((0,), (0,)), ((), ())), preferred_element_type=F32)
            if backward:
                tot = o + of_ref[rows, vc]
                r = lax.rsqrt(jnp.mean(tot * tot, axis=-1, keepdims=True) + NORM_EPS)
                gt = gate_ref[rows, vc].astype(F32)
                out_ref[rows, vc] = (tot * r * onorm_ref[...] * (gt * jax.nn.sigmoid(gt))).astype(BF16)
            else:
                out_ref[rows, vc] = o


def _gla_direction(proj, lr, up, bias, *, batch, lp, backward, o_fwd=None, o_norm=None):
    np_ = proj.shape[0]
    nb = lp // GLA_BLOCK
    kb = GLA_KEY_DIM

    if backward:
        def blk(b, i):
            return b * nb + jnp.where(i == nb - 1, nb - 1, nb - 2 - i)
    else:
        def blk(b, i):
            return b * nb + jnp.where(i == 0, nb - 1, i - 1)

    in_specs = [
        pl.BlockSpec((GLA_BLOCK, kb), lambda b, i: (blk(b, i), 0)),
        pl.BlockSpec((GLA_BLOCK, kb), lambda b, i: (blk(b, i), 1)),
        pl.BlockSpec((GLA_BLOCK, GLA_VALUE_DIM), lambda b, i: (blk(b, i), 1)),
        pl.BlockSpec((GLA_BLOCK, LANES), lambda b, i: (blk(b, i), 0)),
        pl.BlockSpec((GLA_GATE_RANK, kb), lambda b, i: (0, 0)),
        pl.BlockSpec((1, kb), lambda b, i: (0, 0)),
    ]
    args = [proj, proj, proj, lr, up, bias.reshape(1, kb)]
    if backward:
        in_specs += [
            pl.BlockSpec((GLA_BLOCK, GLA_VALUE_DIM), lambda b, i: (blk(b, i), 0)),
            pl.BlockSpec((GLA_BLOCK, GLA_VALUE_DIM), lambda b, i: (blk(b, i), 2)),
            pl.BlockSpec((1, GLA_HEAD_V), lambda b, i: (0, 0)),
        ]
        args += [o_fwd, proj, o_norm.reshape(1, GLA_HEAD_V)]
    return pl.pallas_call(
        functools.partial(_gla_kernel, backward, nb),
        grid=(batch, nb),
        in_specs=in_specs,
        out_specs=pl.BlockSpec((GLA_BLOCK, GLA_VALUE_DIM), lambda b, i: (blk(b, i), 0)),
        out_shape=jax.ShapeDtypeStruct((np_, GLA_VALUE_DIM), BF16 if backward else F32),
        scratch_shapes=[pltpu.VMEM((GLA_HEADS, GLA_HEAD_V, GLA_HEAD_K), F32)],
        compiler_params=pltpu.CompilerParams(
            dimension_semantics=("parallel", "arbitrary"), vmem_limit_bytes=VMEM_LIMIT),
        name="gla_bwd" if backward else "gla_fwd",
    )(*args)


def _rope_tables(s):
    rows = s // GRID_W
    row = jnp.repeat(jnp.arange(rows), GRID_W).astype(F32)
    col = jnp.tile(jnp.arange(GRID_W), rows).astype(F32)
    half = HEAD_DIM // 4
    inv_freq = ROPE_THETA ** (-jnp.arange(0, 2 * half, 2, dtype=F32) / (2 * half))
    ang_row = row[:, None] * inv_freq[None]
    ang_col = col[:, None] * inv_freq[None]
    ang = jnp.concatenate([ang_row, ang_row, ang_col, ang_col], axis=-1)
    ang = jnp.concatenate([ang, jnp.zeros((ROW_PAD, HEAD_DIM), F32)], axis=0)
    cos, sin = jnp.cos(ang), jnp.sin(ang)
    first_half = (jnp.arange(HEAD_DIM) % (2 * half)) < half
    sin_a = jnp.where(first_half[None], -sin, 0.0)
    sin_b = jnp.where(first_half[None], 0.0, sin)
    return cos, sin_a, sin_b


def kernel(x, meta_tokens, pre_norm, post_norm, attn_w_in, attn_q_norm, attn_k_norm,
           attn_w_out, gla_w_in, gla_gk_up, gla_gk_bias, gla_o_norm, gla_w_out):
    batch, s, d = x.shape
    assert d == D_MODEL and s % GLA_BLOCK == 0 and s % KV_TILE == 0 and s % GRID_W == 0
    lp = s + ROW_PAD
    depth = pre_norm.shape[0]

    meta = jnp.broadcast_to(meta_tokens.astype(x.dtype)[None], (batch, N_META, d))
    zeros = jnp.zeros((batch, ROW_PAD - N_META, d), x.dtype)
    h = jnp.concatenate([x, meta, zeros], axis=1).reshape(batch * lp, d)
    tables = _rope_tables(s)

    for i in range(depth):
        j = i // 2
        if i % 2 == 0:
            head_w = jnp.concatenate([
                jnp.tile(attn_q_norm[j] * (HEAD_DIM ** -0.5), ATTN_HEADS),
                jnp.tile(attn_k_norm[j], ATTN_KV_HEADS)]).reshape(1, -1)
            proj = _inproj(h, pre_norm[i], attn_w_in[j].astype(BF16), lp=lp,
                           rope=(head_w,) + tables)
            o = _attention(proj, batch=batch, lp=lp)
            h = _outproj(o, attn_w_out[j].astype(BF16), h, post_norm[i], lp=lp,
                         gate_src=proj, gate_col_block=(ATTN_IN - ATTN_WIDTH) // ATTN_WIDTH)
        else:
            w = gla_w_in[j]
            w_lr = jnp.pad(w[:, GLA_MAIN:], ((0, 0), (0, LANES - 2 * GLA_GATE_RANK)))
            proj, lr = _inproj(h, pre_norm[i], w[:, :GLA_MAIN].astype(BF16), lp=lp,
                               w_lr=w_lr.astype(BF16))
            o_f = _gla_direction(proj, lr, gla_gk_up[j, 0], gla_gk_bias[j, 0],
                                 batch=batch, lp=lp, backward=False)
            gated = _gla_direction(proj, lr, gla_gk_up[j, 1], gla_gk_bias[j, 1],
                                   batch=batch, lp=lp, backward=True,
                                   o_fwd=o_f, o_norm=gla_o_norm[j])
            h = _outproj(gated, gla_w_out[j].astype(BF16), h, post_norm[i], lp=lp)
    return h.reshape(batch, lp, d)[:, :s]
```

```python
import functools

import jax
import jax.numpy as jnp
from jax import lax
from jax.experimental import pallas as pl
from jax.experimental.pallas import tpu as pltpu

F32 = jnp.float32
BF16 = jnp.bfloat16

D_MODEL = 2048
N_META = 16
GRID_W = 64
NORM_EPS = 1e-6
ROPE_THETA = 10000.0
LOG2_E = 1.4426950408889634

HEAD_DIM = 128
ATTN_HEADS = 16
ATTN_KV_HEADS = 8
ATTN_GROUP = ATTN_HEADS // ATTN_KV_HEADS
ATTN_WIDTH = ATTN_HEADS * HEAD_DIM
ATTN_KV_WIDTH = ATTN_KV_HEADS * HEAD_DIM
ATTN_IN = 2 * ATTN_WIDTH + 2 * ATTN_KV_WIDTH

GLA_HEADS = 4
GLA_KEY_DIM = D_MODEL // 2
GLA_VALUE_DIM = D_MODEL
GLA_HEAD_K = GLA_KEY_DIM // GLA_HEADS
GLA_HEAD_V = GLA_VALUE_DIM // GLA_HEADS
GLA_GATE_RANK = 16
GLA_GATE_NORMALIZER = 16.0
GLA_CHUNK = 64
GLA_MAIN = 2 * GLA_KEY_DIM + 2 * GLA_VALUE_DIM

LANES = 128
ROW_PAD = 128
GLA_BLOCK = 128
VMEM_LIMIT = 56 * 1024 * 1024

PROJ_TN = 1024
KV_TILE = 512


def _pick_tile(n, candidates):
    for c in candidates:
        if n % c == 0:
            return c
    raise ValueError(f"no tile in {candidates} divides {n}")


def _split_hi_lo(a):
    hi = a.astype(BF16)
    lo = (a - hi.astype(F32)).astype(BF16)
    return hi, lo


def _inproj_kernel(n_rope_tiles, has_lr, tn, *refs):
    it = iter(refs)
    h_ref, pw_ref, w_ref = next(it), next(it), next(it)
    wlr_ref = next(it) if has_lr else None
    if n_rope_tiles:
        hn_ref, c_ref, sa_ref, sb_ref = next(it), next(it), next(it), next(it)
    o_ref = next(it)
    lr_ref = next(it) if has_lr else None
    xn_sc = next(it)

    j = pl.program_id(1)

    @pl.when(j == 0)
    def _():
        x = h_ref[...]
        ms = jnp.mean(x * x, axis=-1, keepdims=True)
        xn = (x * lax.rsqrt(ms + NORM_EPS) * pw_ref[...]).astype(BF16)
        xn_sc[...] = xn
        if has_lr:
            lr_ref[...] = jnp.dot(xn, wlr_ref[...], preferred_element_type=F32)

    acc = jnp.dot(xn_sc[...], w_ref[...], preferred_element_type=F32)

    if n_rope_tiles:
        @pl.when(j < n_rope_tiles)
        def _():
            c, sa, sb = c_ref[...], sa_ref[...], sb_ref[...]
            for hh in range(tn // HEAD_DIM):
                cols = slice(hh * HEAD_DIM, (hh + 1) * HEAD_DIM)
                a = acc[:, cols]
                r = lax.rsqrt(jnp.mean(a * a, axis=-1, keepdims=True) + NORM_EPS)
                y = a * r * hn_ref[:, cols]
                y = (y * c + pltpu.roll(y, HEAD_DIM - 32, 1) * sa + pltpu.roll(y, 32, 1) * sb)
                o_ref[:, cols] = y.astype(BF16)

        @pl.when(j >= n_rope_tiles)
        def _():
            o_ref[...] = acc.astype(BF16)
    else:
        o_ref[...] = acc.astype(BF16)


def _inproj(h, pre_w, w, *, lp, w_lr=None, rope=None):
    np_, d = h.shape
    n = w.shape[1]
    tm = _pick_tile(lp, (704, 384, 128))
    tn = PROJ_TN
    tiles_per_batch = lp // tm
    has_lr = w_lr is not None
    n_rope_tiles = 0
    in_specs = [
        pl.BlockSpec((tm, d), lambda i, j: (i, 0)),
        pl.BlockSpec((1, d), lambda i, j: (0, 0)),
        pl.BlockSpec((d, tn), lambda i, j: (0, j)),
    ]
    args = [h, pre_w.reshape(1, d), w]
    if has_lr:
        in_specs.append(pl.BlockSpec((d, LANES), lambda i, j: (0, 0)))
        args.append(w_lr)
    if rope is not None:
        head_w, cos_t, sa_t, sb_t = rope
        n_rope_tiles = head_w.shape[1] // tn
        last = n_rope_tiles - 1
        in_specs.append(pl.BlockSpec((1, tn), lambda i, j: (0, jnp.minimum(j, last))))
        tab = pl.BlockSpec((tm, HEAD_DIM), lambda i, j: (i % tiles_per_batch, 0))
        in_specs += [tab, tab, tab]
        args += [head_w, cos_t, sa_t, sb_t]
    out_shape = [jax.ShapeDtypeStruct((np_, n), BF16)]
    out_specs = [pl.BlockSpec((tm, tn), lambda i, j: (i, j))]
    if has_lr:
        out_shape.append(jax.ShapeDtypeStruct((np_, LANES), F32))
        out_specs.append(pl.BlockSpec((tm, LANES), lambda i, j: (i, 0)))
    res = pl.pallas_call(
        functools.partial(_inproj_kernel, n_rope_tiles, has_lr, tn),
        grid=(np_ // tm, n // tn),
        in_specs=in_specs,
        out_specs=out_specs,
        out_shape=out_shape,
        scratch_shapes=[pltpu.VMEM((tm, d), BF16)],
        compiler_params=pltpu.CompilerParams(
            dimension_semantics=("parallel", "arbitrary"), vmem_limit_bytes=VMEM_LIMIT),
        name="inproj_rope" if rope is not None else "inproj_gla",
    )(*args)
    return res if has_lr else res[0]


def _attn_kernel(s_real, tq, q_ref, k_ref, v_ref, o_ref, vt_sc, m_sc, l_sc, acc_sc,
                 s_a, s_b, cm_a, cm_b):
    lp = s_real + ROW_PAD
    s_bufs, cm_bufs = (s_a, s_b), (cm_a, cm_b)

    @pl.when(pl.program_id(2) == 0)
    def _():
        for c in range(lp // LANES):
            cols = slice(c * LANES, (c + 1) * LANES)
            vt_sc[:, cols] = v_ref[cols, :].astype(F32).T.astype(BF16)

    q = jnp.concatenate([q_ref[:, g * HEAD_DIM:(g + 1) * HEAD_DIM] for g in range(ATTN_GROUP)],
                        axis=0)
    m_sc[...] = jnp.full_like(m_sc, -jnp.inf)
    l_sc[...] = jnp.zeros_like(l_sc)
    acc_sc[...] = jnp.zeros_like(acc_sc)

    chunks = [(c * KV_TILE, KV_TILE, None) for c in range(s_real // KV_TILE)]
    chunks.append((s_real, ROW_PAD, N_META))

    def scores(idx):
        r0, rows, n_valid = chunks[idx]
        s_buf, cm_buf = s_bufs[idx % 2], cm_bufs[idx % 2]
        s = lax.dot_general(k_ref[r0:r0 + rows, :], q, (((1,), (1,)), ((), ())),
                            preferred_element_type=F32)
        if n_valid is not None:
            row = lax.broadcasted_iota(jnp.int32, s.shape, 0)
            s = jnp.where(row < n_valid, s, -jnp.inf)
        s_buf[0:rows, :] = s
        cm_buf[...] = jnp.max(s, axis=0, keepdims=True)

    def accumulate(idx):
        r0, rows, _ = chunks[idx]
        s_buf, cm_buf = s_bufs[idx % 2], cm_bufs[idx % 2]
        m_prev = m_sc[...]
        m_new = jnp.maximum(m_prev, cm_buf[...])
        alpha = jnp.exp2(m_prev - m_new)
        p = jnp.exp2(s_buf[0:rows, :] - m_new)
        l_sc[...] = alpha * l_sc[...] + jnp.sum(p, axis=0, keepdims=True)
        acc_sc[...] = alpha * acc_sc[...] + jnp.dot(vt_sc[:, r0:r0 + rows], p.astype(BF16),
                                                    preferred_element_type=F32)
        m_sc[...] = m_new

    scores(0)
    for c in range(len(chunks)):
        if c + 1 < len(chunks):
            scores(c + 1)
        accumulate(c)

    o = (acc_sc[...] / l_sc[...]).T
    for g in range(ATTN_GROUP):
        o_ref[:, g * HEAD_DIM:(g + 1) * HEAD_DIM] = o[g * tq:(g + 1) * tq].astype(BF16)


def _attention(proj, *, batch, lp):
    np_ = proj.shape[0]
    s_real = lp - ROW_PAD
    tq = _pick_tile(lp, (384, 128))
    nq = lp // tq
    gw = ATTN_GROUP * HEAD_DIM
    k_col0 = ATTN_WIDTH // HEAD_DIM
    v_col0 = (ATTN_WIDTH + ATTN_KV_WIDTH) // HEAD_DIM
    return pl.pallas_call(
        functools.partial(_attn_kernel, s_real, tq),
        grid=(batch, ATTN_KV_HEADS, nq),
        in_specs=[
            pl.BlockSpec((tq, gw), lambda b, h, i: (b * nq + i, h)),
            pl.BlockSpec((lp, HEAD_DIM), lambda b, h, i: (b, k_col0 + h)),
            pl.BlockSpec((lp, HEAD_DIM), lambda b, h, i: (b, v_col0 + h)),
        ],
        out_specs=pl.BlockSpec((tq, gw), lambda b, h, i: (b * nq + i, h)),
        out_shape=jax.ShapeDtypeStruct((np_, ATTN_WIDTH), BF16),
        scratch_shapes=[
            pltpu.VMEM((HEAD_DIM, lp), BF16),
            pltpu.VMEM((1, ATTN_GROUP * tq), F32),
            pltpu.VMEM((1, ATTN_GROUP * tq), F32),
            pltpu.VMEM((HEAD_DIM, ATTN_GROUP * tq), F32),
            pltpu.VMEM((KV_TILE, ATTN_GROUP * tq), F32),
            pltpu.VMEM((KV_TILE, ATTN_GROUP * tq), F32),
            pltpu.VMEM((1, ATTN_GROUP * tq), F32),
            pltpu.VMEM((1, ATTN_GROUP * tq), F32),
        ],
        compiler_params=pltpu.CompilerParams(
            dimension_semantics=("parallel", "parallel", "arbitrary"), vmem_limit_bytes=VMEM_LIMIT),
        name="attention",
    )(proj, proj, proj)


def _outproj_kernel(has_gate, *refs):
    it = iter(refs)
    x_ref = next(it)
    g_ref = next(it) if has_gate else None
    w_ref, h_ref, pw_ref, out_ref = next(it), next(it), next(it), next(it)
    x = x_ref[...]
    if has_gate:
        g = g_ref[...].astype(F32)
        x = (x.astype(F32) * (g * jax.nn.sigmoid(g))).astype(BF16)
    y = jnp.dot(x, w_ref[...], preferred_element_type=F32)
    r = lax.rsqrt(jnp.mean(y * y, axis=-1, keepdims=True) + NORM_EPS)
    out_ref[...] = h_ref[...] + y * r * pw_ref[...]


def _outproj(x, w, h, post_w, *, lp, gate_src=None, gate_col_block=None):
    np_, kdim = x.shape
    d = w.shape[1]
    tm = _pick_tile(lp, (384, 128))
    has_gate = gate_src is not None
    in_specs = [pl.BlockSpec((tm, kdim), lambda i: (i, 0))]
    args = [x]
    if has_gate:
        in_specs.append(pl.BlockSpec((tm, kdim), lambda i: (i, gate_col_block)))
        args.append(gate_src)
    in_specs += [
        pl.BlockSpec((kdim, d), lambda i: (0, 0)),
        pl.BlockSpec((tm, d), lambda i: (i, 0)),
        pl.BlockSpec((1, d), lambda i: (0, 0)),
    ]
    args += [w, h, post_w.reshape(1, d)]
    return pl.pallas_call(
        functools.partial(_outproj_kernel, has_gate),
        grid=(np_ // tm,),
        in_specs=in_specs,
        out_specs=pl.BlockSpec((tm, d), lambda i: (i, 0)),
        out_shape=jax.ShapeDtypeStruct((np_, d), F32),
        compiler_params=pltpu.CompilerParams(
            dimension_semantics=("parallel",), vmem_limit_bytes=VMEM_LIMIT),
        name="outproj_gated" if has_gate else "outproj",
    )(*args)


def _gla_kernel(backward, n_blocks, *refs):
    it = iter(refs)
    q_ref, k_ref, v_ref, lr_ref, up_ref, bias_ref = (next(it) for _ in range(6))
    if backward:
        of_ref, gate_ref, onorm_ref = next(it), next(it), next(it)
    out_ref, st_sc = next(it), next(it)

    i = pl.program_id(1)

    @pl.when(i == 0)
    def _():
        st_sc[...] = jnp.zeros_like(st_sc)

    is_meta = (i == n_blocks - 1) if backward else (i == 0)
    n_valid = jnp.where(is_meta, N_META, GLA_BLOCK)

    C = GLA_CHUNK
    row = lax.broadcasted_iota(jnp.int32, (C, C), 0)
    col = lax.broadcasted_iota(jnp.int32, (C, C), 1)
    if backward:
        cum_mat = (col >= row).astype(BF16)
        pair_mask = col > row
    else:
        cum_mat = (col <= row).astype(BF16)
        pair_mask = col <= row
    edge = 0 if backward else C - 1
    scale = GLA_HEAD_K ** -0.5
    gate_cols = slice(GLA_GATE_RANK, 2 * GLA_GATE_RANK) if backward else slice(0, GLA_GATE_RANK)

    for sub in ((1, 0) if backward else (0, 1)):
        r0 = sub * C
        rows = slice(r0, r0 + C)
        valid = (lax.broadcasted_iota(jnp.int32, (C, 1), 0) + r0) < n_valid
        lr_hi, lr_lo = _split_hi_lo(lr_ref[rows, gate_cols])
        for hd in range(GLA_HEADS):
            kc = slice(hd * GLA_HEAD_K, (hd + 1) * GLA_HEAD_K)
            vc = slice(hd * GLA_HEAD_V, (hd + 1) * GLA_HEAD_V)
            up_hi, up_lo = _split_hi_lo(up_ref[:, kc])
            z = (jnp.dot(lr_hi, up_hi, preferred_element_type=F32)
                 + jnp.dot(lr_hi, up_lo, preferred_element_type=F32)
                 + jnp.dot(lr_lo, up_hi, preferred_element_type=F32)) + bias_ref[:, kc]
            g = jnp.where(valid, jax.nn.log_sigmoid(z) / GLA_GATE_NORMALIZER, 0.0)
            g_hi, g_lo = _split_hi_lo(g)
            b = (jnp.dot(cum_mat, g_hi, preferred_element_type=F32)
                 + jnp.dot(cum_mat, g_lo, preferred_element_type=F32))
            b_edge = b[edge:edge + 1, :]
            q = q_ref[rows, kc].astype(F32)
            k = jnp.where(valid, k_ref[rows, kc].astype(F32), 0.0)
            v = jnp.where(valid, v_ref[rows, vc], jnp.zeros((), BF16))
            q_dec = (q * jnp.exp(b) * scale).astype(BF16)
            k_inv = (k * jnp.exp(-b)).astype(BF16)
            k_end = (k * jnp.exp(b_edge - b)).astype(BF16)
            decay = jnp.exp(b_edge)
            sc = lax.dot_general(q_dec, k_inv, (((1,), (1,)), ((), ())),
                                 preferred_element_type=F32)
            sc = jnp.where(pair_mask, sc, 0.0).astype(BF16)
            st = st_sc[hd]
            o = (jnp.dot(sc, v, preferred_element_type=F32)
                 + lax.dot_general(q_dec, st.astype(BF16), (((1,), (1,)), ((), ())),
                                   preferred_element_type=F32))
            st_sc[hd] = st * decay + lax.dot_general(
                v, k_end, (((0,), (0,)), ((), ())), preferred_element_type=F32)
            if backward:
                tot = o + of_ref[rows, vc]
                r = lax.rsqrt(jnp.mean(tot * tot, axis=-1, keepdims=True) + NORM_EPS)
                gt = gate_ref[rows, vc].astype(F32)
                out_ref[rows, vc] = (tot * r * onorm_ref[...] * (gt * jax.nn.sigmoid(gt))).astype(BF16)
            else:
                out_ref[rows, vc] = o


def _gla_direction(proj, lr, up, bias, *, batch, lp, backward, o_fwd=None, o_norm=None):
    np_ = proj.shape[0]
    nb = lp // GLA_BLOCK
    kb = GLA_KEY_DIM

    if backward:
        def blk(b, i):
            return b * nb + jnp.where(i == nb - 1, nb - 1, nb - 2 - i)
    else:
        def blk(b, i):
            return b * nb + jnp.where(i == 0, nb - 1, i - 1)

    in_specs = [
        pl.BlockSpec((GLA_BLOCK, kb), lambda b, i: (blk(b, i), 0)),
        pl.BlockSpec((GLA_BLOCK, kb), lambda b, i: (blk(b, i), 1)),
        pl.BlockSpec((GLA_BLOCK, GLA_VALUE_DIM), lambda b, i: (blk(b, i), 1)),
        pl.BlockSpec((GLA_BLOCK, LANES), lambda b, i: (blk(b, i), 0)),
        pl.BlockSpec((GLA_GATE_RANK, kb), lambda b, i: (0, 0)),
        pl.BlockSpec((1, kb), lambda b, i: (0, 0)),
    ]
    args = [proj, proj, proj, lr, up, bias.reshape(1, kb)]
    if backward:
        in_specs += [
            pl.BlockSpec((GLA_BLOCK, GLA_VALUE_DIM), lambda b, i: (blk(b, i), 0)),
            pl.BlockSpec((GLA_BLOCK, GLA_VALUE_DIM), lambda b, i: (blk(b, i), 2)),
            pl.BlockSpec((1, GLA_HEAD_V), lambda b, i: (0, 0)),
        ]
        args += [o_fwd, proj, o_norm.reshape(1, GLA_HEAD_V)]
    return pl.pallas_call(
        functools.partial(_gla_kernel, backward, nb),
        grid=(batch, nb),
        in_specs=in_specs,
        out_specs=pl.BlockSpec((GLA_BLOCK, GLA_VALUE_DIM), lambda b, i: (blk(b, i), 0)),
        out_shape=jax.ShapeDtypeStruct((np_, GLA_VALUE_DIM), BF16 if backward else F32),
        scratch_shapes=[pltpu.VMEM((GLA_HEADS, GLA_HEAD_V, GLA_HEAD_K), F32)],
        compiler_params=pltpu.CompilerParams(
            dimension_semantics=("parallel", "arbitrary"), vmem_limit_bytes=VMEM_LIMIT),
        name="gla_bwd" if backward else "gla_fwd",
    )(*args)


def _rope_tables(s):
    rows = s // GRID_W
    row = jnp.repeat(jnp.arange(rows), GRID_W).astype(F32)
    col = jnp.tile(jnp.arange(GRID_W), rows).astype(F32)
    half = HEAD_DIM // 4
    inv_freq = ROPE_THETA ** (-jnp.arange(0, 2 * half, 2, dtype=F32) / (2 * half))
    ang_row = row[:, None] * inv_freq[None]
    ang_col = col[:, None] * inv_freq[None]
    ang = jnp.concatenate([ang_row, ang_row, ang_col, ang_col], axis=-1)
    ang = jnp.concatenate([ang, jnp.zeros((ROW_PAD, HEAD_DIM), F32)], axis=0)
    cos, sin = jnp.cos(ang), jnp.sin(ang)
    first_half = (jnp.arange(HEAD_DIM) % (2 * half)) < half
    sin_a = jnp.where(first_half[None], -sin, 0.0)
    sin_b = jnp.where(first_half[None], 0.0, sin)
    return cos, sin_a, sin_b


def kernel(x, meta_tokens, pre_norm, post_norm, attn_w_in, attn_q_norm, attn_k_norm,
           attn_w_out, gla_w_in, gla_gk_up, gla_gk_bias, gla_o_norm, gla_w_out):
    batch, s, d = x.shape
    assert d == D_MODEL and s % GLA_BLOCK == 0 and s % KV_TILE == 0 and s % GRID_W == 0
    lp = s + ROW_PAD
    depth = pre_norm.shape[0]

    meta = jnp.broadcast_to(meta_tokens.astype(x.dtype)[None], (batch, N_META, d))
    zeros = jnp.zeros((batch, ROW_PAD - N_META, d), x.dtype)
    h = jnp.concatenate([x, meta, zeros], axis=1).reshape(batch * lp, d)
    tables = _rope_tables(s)

    for i in range(depth):
        j = i // 2
        if i % 2 == 0:
            head_w = jnp.concatenate([
                jnp.tile(attn_q_norm[j] * (HEAD_DIM ** -0.5 * LOG2_E), ATTN_HEADS),
                jnp.tile(attn_k_norm[j], ATTN_KV_HEADS)]).reshape(1, -1)
            proj = _inproj(h, pre_norm[i], attn_w_in[j].astype(BF16), lp=lp,
                           rope=(head_w,) + tables)
            o = _attention(proj, batch=batch, lp=lp)
            h = _outproj(o, attn_w_out[j].astype(BF16), h, post_norm[i], lp=lp,
                         gate_src=proj, gate_col_block=(ATTN_IN - ATTN_WIDTH) // ATTN_WIDTH)
        else:
            w = gla_w_in[j]
            w_lr = jnp.pad(w[:, GLA_MAIN:], ((0, 0), (0, LANES - 2 * GLA_GATE_RANK)))
            proj, lr = _inproj(h, pre_norm[i], w[:, :GLA_MAIN].astype(BF16), lp=lp,
                               w_lr=w_lr.astype(BF16))
            o_f = _gla_direction(proj, lr, gla_gk_up[j, 0], gla_gk_bias[j, 0],
                                 batch=batch, lp=lp, backward=False)
            gated = _gla_direction(proj, lr, gla_gk_up[j, 1], gla_gk_bias[j, 1],
                                   batch=batch, lp=lp, backward=True,
                                   o_fwd=o_f, o_norm=gla_o_norm[j])
            h = _outproj(gated, gla_w_out[j].astype(BF16), h, post_norm[i], lp=lp)
    return h.reshape(batch, lp, d)[:, :s]
```

```python
import functools

import jax
import jax.numpy as jnp
from jax import lax
from jax.experimental import pallas as pl
from jax.experimental.pallas import tpu as pltpu

F32 = jnp.float32
BF16 = jnp.bfloat16

D_MODEL = 2048
N_META = 16
GRID_W = 64
NORM_EPS = 1e-6
ROPE_THETA = 10000.0
LOG2_E = 1.4426950408889634

HEAD_DIM = 128
ATTN_HEADS = 16
ATTN_KV_HEADS = 8
ATTN_GROUP = ATTN_HEADS // ATTN_KV_HEADS
ATTN_WIDTH = ATTN_HEADS * HEAD_DIM
ATTN_KV_WIDTH = ATTN_KV_HEADS * HEAD_DIM
ATTN_IN = 2 * ATTN_WIDTH + 2 * ATTN_KV_WIDTH

GLA_HEADS = 4
GLA_KEY_DIM = D_MODEL // 2
GLA_VALUE_DIM = D_MODEL
GLA_HEAD_K = GLA_KEY_DIM // GLA_HEADS
GLA_HEAD_V = GLA_VALUE_DIM // GLA_HEADS
GLA_GATE_RANK = 16
GLA_GATE_NORMALIZER = 16.0
GLA_CHUNK = 64
GLA_MAIN = 2 * GLA_KEY_DIM + 2 * GLA_VALUE_DIM
GLA_IN = GLA_MAIN + 2 * GLA_GATE_RANK

LANES = 128
ROW_PAD = 128
GLA_BLOCK = 128
VMEM_LIMIT = 56 * 1024 * 1024

PROJ_TN = 1024
KV_TILE = 1024
SCORE_BOUND = 60.0


def _pick_tile(n, candidates):
    for c in candidates:
        if n % c == 0:
            return c
    raise ValueError(f"no tile in {candidates} divides {n}")


def _split_hi_lo(a):
    hi = a.astype(BF16)
    lo = (a - hi.astype(F32)).astype(BF16)
    return hi, lo


def _inproj_kernel(n_rope_tiles, has_lr, tn, *refs):
    it = iter(refs)
    h_ref, pw_ref, w_ref = next(it), next(it), next(it)
    wlr_ref = next(it) if has_lr else None
    if n_rope_tiles:
        hn_ref, c_ref, sn_ref = next(it), next(it), next(it)
    o_ref = next(it)
    lr_ref = next(it) if has_lr else None
    xn_sc = next(it)

    j = pl.program_id(1)

    @pl.when(j == 0)
    def _():
        x = h_ref[...]
        ms = jnp.mean(x * x, axis=-1, keepdims=True)
        xn = (x * lax.rsqrt(ms + NORM_EPS) * pw_ref[...]).astype(BF16)
        xn_sc[...] = xn
        if has_lr:
            lr_ref[...] = jnp.dot(xn, wlr_ref[...], preferred_element_type=F32)

    acc = jnp.dot(xn_sc[...], w_ref[...], preferred_element_type=F32)

    if n_rope_tiles:
        @pl.when(j < n_rope_tiles)
        def _():
            c, sn = c_ref[...], sn_ref[...]
            for hh in range(tn // HEAD_DIM):
                cols = slice(hh * HEAD_DIM, (hh + 1) * HEAD_DIM)
                a = acc[:, cols]
                r = lax.rsqrt(jnp.mean(a * a, axis=-1, keepdims=True) + NORM_EPS)
                y = a * r * hn_ref[:, cols]
                y = y * c + pltpu.roll(y, HEAD_DIM // 2, 1) * sn
                o_ref[:, cols] = y.astype(BF16)

        @pl.when(j >= n_rope_tiles)
        def _():
            o_ref[...] = acc.astype(BF16)
    else:
        o_ref[...] = acc.astype(BF16)


def _inproj(h, pre_w, w, n, *, lp, w_lr=None, rope=None):
    np_, d = h.shape
    tm = _pick_tile(lp, (704, 384, 128))
    tn = PROJ_TN
    tiles_per_batch = lp // tm
    has_lr = w_lr is not None
    n_rope_tiles = 0
    in_specs = [
        pl.BlockSpec((tm, d), lambda i, j: (i, 0)),
        pl.BlockSpec((1, d), lambda i, j: (0, 0)),
        pl.BlockSpec((d, tn), lambda i, j: (0, j)),
    ]
    args = [h, pre_w.reshape(1, d), w]
    if has_lr:
        in_specs.append(pl.BlockSpec((d, LANES), lambda i, j: (0, 0)))
        args.append(w_lr)
    if rope is not None:
        head_w, cos_t, sin_t = rope
        n_rope_tiles = head_w.shape[1] // tn
        last = n_rope_tiles - 1
        in_specs.append(pl.BlockSpec((1, tn), lambda i, j: (0, jnp.minimum(j, last))))
        tab = pl.BlockSpec((tm, HEAD_DIM), lambda i, j: (i % tiles_per_batch, 0))
        in_specs += [tab, tab]
        args += [head_w, cos_t, sin_t]
    out_shape = [jax.ShapeDtypeStruct((np_, n), BF16)]
    out_specs = [pl.BlockSpec((tm, tn), lambda i, j: (i, j))]
    if has_lr:
        out_shape.append(jax.ShapeDtypeStruct((np_, LANES), F32))
        out_specs.append(pl.BlockSpec((tm, LANES), lambda i, j: (i, 0)))
    res = pl.pallas_call(
        functools.partial(_inproj_kernel, n_rope_tiles, has_lr, tn),
        grid=(np_ // tm, n // tn),
        in_specs=in_specs,
        out_specs=out_specs,
        out_shape=out_shape,
        scratch_shapes=[pltpu.VMEM((tm, d), BF16)],
        compiler_params=pltpu.CompilerParams(
            dimension_semantics=("parallel", "arbitrary"), vmem_limit_bytes=VMEM_LIMIT),
        name="inproj_rope" if rope is not None else "inproj_gla",
    )(*args)
    return res if has_lr else res[0]


def _attn_kernel(s_real, tq, bounded_ref, q_ref, k_ref, v_ref, o_ref, vt_sc, m_sc, l_sc, acc_sc,
                 s_a, s_b, cm_a, cm_b):
    lp = s_real + ROW_PAD
    s_bufs, cm_bufs = (s_a, s_b), (cm_a, cm_b)

    @pl.when(pl.program_id(2) == 0)
    def _():
        for c in range(lp // LANES):
            cols = slice(c * LANES, (c + 1) * LANES)
            vt_sc[:, cols] = v_ref[cols, :].astype(F32).T.astype(BF16)

    q = jnp.concatenate([q_ref[:, g * HEAD_DIM:(g + 1) * HEAD_DIM] for g in range(ATTN_GROUP)],
                        axis=0)
    n_full = s_real // KV_TILE
    chunks = [(c * KV_TILE, KV_TILE, None) for c in range(n_full - 1)]
    chunks.append(((n_full - 1) * KV_TILE, KV_TILE + ROW_PAD, KV_TILE + N_META))

    def key_scores(idx):
        r0, rows, _ = chunks[idx]
        return lax.dot_general(k_ref[r0:r0 + rows, :], q, (((1,), (1,)), ((), ())),
                               preferred_element_type=F32)

    def mask_tail(idx, x, fill):
        _, rows, n_valid = chunks[idx]
        if n_valid is None:
            return x
        row = lax.broadcasted_iota(jnp.int32, (ROW_PAD, 1), 0) + (rows - ROW_PAD)
        tail = jnp.where(row < n_valid, x[rows - ROW_PAD:], fill)
        return jnp.concatenate([x[:rows - ROW_PAD], tail], axis=0)

    @pl.when(bounded_ref[0] != 0)
    def _():
        acc = jnp.zeros(acc_sc.shape, F32)
        l = jnp.zeros(l_sc.shape, F32)
        for idx, (r0, rows, _) in enumerate(chunks):
            p = mask_tail(idx, jnp.exp2(key_scores(idx)), 0.0)
            l = l + jnp.sum(p, axis=0, keepdims=True)
            acc = acc + jnp.dot(vt_sc[:, r0:r0 + rows], p.astype(BF16),
                                preferred_element_type=F32)
        acc_sc[...] = acc
        l_sc[...] = l

    @pl.when(bounded_ref[0] == 0)
    def _():
        m_sc[...] = jnp.full_like(m_sc, -jnp.inf)
        l_sc[...] = jnp.zeros_like(l_sc)
        acc_sc[...] = jnp.zeros_like(acc_sc)

        def scores(idx):
            rows = chunks[idx][1]
            s = mask_tail(idx, key_scores(idx), -jnp.inf)
            s_bufs[idx % 2][0:rows, :] = s
            cm_bufs[idx % 2][...] = jnp.max(s, axis=0, keepdims=True)

        def accumulate(idx):
            r0, rows, _ = chunks[idx]
            m_prev = m_sc[...]
            m_new = jnp.maximum(m_prev, cm_bufs[idx % 2][...])
            alpha = jnp.exp2(m_prev - m_new)
            p = jnp.exp2(s_bufs[idx % 2][0:rows, :] - m_new)
            l_sc[...] = alpha * l_sc[...] + jnp.sum(p, axis=0, keepdims=True)
            acc_sc[...] = alpha * acc_sc[...] + jnp.dot(
                vt_sc[:, r0:r0 + rows], p.astype(BF16), preferred_element_type=F32)
            m_sc[...] = m_new

        scores(0)
        for c in range(len(chunks)):
            if c + 1 < len(chunks):
                scores(c + 1)
            accumulate(c)

    o = (acc_sc[...] / l_sc[...]).T
    for g in range(ATTN_GROUP):
        o_ref[:, g * HEAD_DIM:(g + 1) * HEAD_DIM] = o[g * tq:(g + 1) * tq].astype(BF16)


def _attention(proj, bounded, *, batch, lp):
    np_ = proj.shape[0]
    s_real = lp - ROW_PAD
    tq = _pick_tile(lp, (384, 128))
    nq = lp // tq
    gw = ATTN_GROUP * HEAD_DIM
    k_col0 = ATTN_WIDTH // HEAD_DIM
    v_col0 = (ATTN_WIDTH + ATTN_KV_WIDTH) // HEAD_DIM
    return pl.pallas_call(
        functools.partial(_attn_kernel, s_real, tq),
        grid=(batch, ATTN_KV_HEADS, nq),
        in_specs=[
            pl.BlockSpec(memory_space=pltpu.SMEM),
            pl.BlockSpec((tq, gw), lambda b, h, i: (b * nq + i, h)),
            pl.BlockSpec((lp, HEAD_DIM), lambda b, h, i: (b, k_col0 + h)),
            pl.BlockSpec((lp, HEAD_DIM), lambda b, h, i: (b, v_col0 + h)),
        ],
        out_specs=pl.BlockSpec((tq, gw), lambda b, h, i: (b * nq + i, h)),
        out_shape=jax.ShapeDtypeStruct((np_, ATTN_WIDTH), BF16),
        scratch_shapes=[
            pltpu.VMEM((HEAD_DIM, lp), BF16),
            pltpu.VMEM((1, ATTN_GROUP * tq), F32),
            pltpu.VMEM((1, ATTN_GROUP * tq), F32),
            pltpu.VMEM((HEAD_DIM, ATTN_GROUP * tq), F32),
            pltpu.VMEM((KV_TILE + ROW_PAD, ATTN_GROUP * tq), F32),
            pltpu.VMEM((KV_TILE + ROW_PAD, ATTN_GROUP * tq), F32),
            pltpu.VMEM((1, ATTN_GROUP * tq), F32),
            pltpu.VMEM((1, ATTN_GROUP * tq), F32),
        ],
        compiler_params=pltpu.CompilerParams(
            dimension_semantics=("parallel", "parallel", "arbitrary"), vmem_limit_bytes=VMEM_LIMIT),
        name="attention",
    )(bounded, proj, proj, proj)


def _outproj_kernel(has_gate, *refs):
    it = iter(refs)
    x_ref = next(it)
    g_ref = next(it) if has_gate else None
    w_ref, h_ref, pw_ref, out_ref = next(it), next(it), next(it), next(it)
    x = x_ref[...]
    if has_gate:
        g = g_ref[...].astype(F32)
        x = (x.astype(F32) * (g * jax.nn.sigmoid(g))).astype(BF16)
    y = jnp.dot(x, w_ref[...], preferred_element_type=F32)
    r = lax.rsqrt(jnp.mean(y * y, axis=-1, keepdims=True) + NORM_EPS)
    out_ref[...] = h_ref[...] + y * r * pw_ref[...]


def _outproj(x, w, h, post_w, *, lp, gate_src=None, gate_col_block=None):
    np_, kdim = x.shape
    d = w.shape[1]
    tm = _pick_tile(lp, (384, 128))
    has_gate = gate_src is not None
    in_specs = [pl.BlockSpec((tm, kdim), lambda i: (i, 0))]
    args = [x]
    if has_gate:
        in_specs.append(pl.BlockSpec((tm, kdim), lambda i: (i, gate_col_block)))
        args.append(gate_src)
    in_specs += [
        pl.BlockSpec((kdim, d), lambda i: (0, 0)),
        pl.BlockSpec((tm, d), lambda i: (i, 0)),
        pl.BlockSpec((1, d), lambda i: (0, 0)),
    ]
    args += [w, h, post_w.reshape(1, d)]
    return pl.pallas_call(
        functools.partial(_outproj_kernel, has_gate),
        grid=(np_ // tm,),
        in_specs=in_specs,
        out_specs=pl.BlockSpec((tm, d), lambda i: (i, 0)),
        out_shape=jax.ShapeDtypeStruct((np_, d), F32),
        compiler_params=pltpu.CompilerParams(
            dimension_semantics=("parallel",), vmem_limit_bytes=VMEM_LIMIT),
        name="outproj_gated" if has_gate else "outproj",
    )(*args)


def _gla_kernel(backward, n_blocks, *refs):
    it = iter(refs)
    q_ref, k_ref, v_ref, lr_ref, up_ref, bias_ref = (next(it) for _ in range(6))
    if backward:
        of_ref, gate_ref, onorm_ref = next(it), next(it), next(it)
    out_ref, st_sc = next(it), next(it)

    i = pl.program_id(1)

    @pl.when(i == 0)
    def _():
        st_sc[...] = jnp.zeros_like(st_sc)

    is_meta = (i == n_blocks - 1) if backward else (i == 0)
    n_valid = jnp.where(is_meta, N_META, GLA_BLOCK)

    C, R = GLA_CHUNK, GLA_BLOCK
    row = lax.broadcasted_iota(jnp.int32, (R, R), 0)
    col = lax.broadcasted_iota(jnp.int32, (R, R), 1)
    same_chunk = (row // C) == (col // C)
    if backward:
        cum_mat = (same_chunk & (col >= row)).astype(BF16)
        own_mask = same_chunk & (col > row)
        cross_mask = (row < C) & (col >= C)
        e0, e1 = 0, C
    else:
        cum_mat = (same_chunk & (col <= row)).astype(BF16)
        own_mask = same_chunk & (col <= row)
        cross_mask = (row >= C) & (col < C)
        e0, e1 = C - 1, R - 1
    gate_cols = slice(GLA_GATE_RANK, 2 * GLA_GATE_RANK) if backward else slice(0, GLA_GATE_RANK)

    rows1 = lax.broadcasted_iota(jnp.int32, (R, 1), 0)
    valid = rows1 < n_valid
    first = rows1 < C

    z = jnp.dot(lr_ref[:, gate_cols].astype(BF16), up_ref[...].astype(BF16),
                preferred_element_type=F32) + bias_ref[...]
    log_sig = jnp.minimum(z, 0.0) - jnp.log1p(jnp.exp(-jnp.abs(z)))
    g = jnp.where(valid, log_sig / GLA_GATE_NORMALIZER, 0.0)
    g_hi, g_lo = _split_hi_lo(g)
    b = (jnp.dot(cum_mat, g_hi, preferred_element_type=F32)
         + jnp.dot(cum_mat, g_lo, preferred_element_type=F32))
    tot0, tot1 = b[e0:e0 + 1, :], b[e1:e1 + 1, :]
    tot_own = jnp.where(first, tot0, tot1)
    q = q_ref[...].astype(F32)
    k = jnp.where(valid, k_ref[...].astype(F32), 0.0)
    v_all = jnp.where(valid, v_ref[...], jnp.zeros((), BF16))
    qd = q * jnp.exp(b)
    ki = (k * jnp.exp(-b)).astype(BF16)
    ke = k * jnp.exp(tot_own - b)
    if backward:
        q_st = jnp.concatenate([qd[:C] * jnp.exp(tot1), qd[C:]], axis=0)
        ke_st = jnp.concatenate([ke[:C], ke[C:] * jnp.exp(tot0)], axis=0)
    else:
        q_st = jnp.concatenate([qd[:C], qd[C:] * jnp.exp(tot0)], axis=0)
        ke_st = jnp.concatenate([ke[:C] * jnp.exp(tot1), ke[C:]], axis=0)
    qd, ke = qd.astype(BF16), ke.astype(BF16)
    q_st, ke_st = q_st.astype(BF16), ke_st.astype(BF16)
    decay = jnp.exp(tot0 + tot1)

    for hd in range(GLA_HEADS):
        kc = slice(hd * GLA_HEAD_K, (hd + 1) * GLA_HEAD_K)
        vc = slice(hd * GLA_HEAD_V, (hd + 1) * GLA_HEAD_V)
        v = v_all[:, vc]
        keys = jnp.concatenate([ki[:, kc], ke[:, kc]], axis=0)
        p = lax.dot_general(qd[:, kc], keys, (((1,), (1,)), ((), ())),
                            preferred_element_type=F32)
        sc = (jnp.where(own_mask, p[:, :R], 0.0)
              + jnp.where(cross_mask, p[:, R:], 0.0)).astype(BF16)
        st = st_sc[hd]
        o = (jnp.dot(sc, v, preferred_element_type=F32)
             + lax.dot_general(q_st[:, kc], st.astype(BF16), (((1,), (1,)), ((), ())),
                               preferred_element_type=F32))
        st_sc[hd] = st * decay[:, kc] + lax.dot_general(
            v, ke_st[:, kc], (((0,), (0,)), ((), ())), preferred_element_type=F32)
        if backward:
            tot = o + of_ref[:, vc]
            r = lax.rsqrt(jnp.mean(tot * tot, axis=-1, keepdims=True) + NORM_EPS)
            gt = gate_ref[:, vc].astype(F32)
            out_ref[:, vc] = (tot * r * onorm_ref[...] * (gt * jax.nn.sigmoid(gt))).astype(BF16)
        else:
            out_ref[:, vc] = o


def _gla_direction(proj, lr, up, bias, *, batch, lp, backward, o_fwd=None, o_norm=None):
    np_ = proj.shape[0]
    nb = lp // GLA_BLOCK
    kb = GLA_KEY_DIM

    if backward:
        def blk(b, i):
            return b * nb + jnp.where(i == nb - 1, nb - 1, nb - 2 - i)
    else:
        def blk(b, i):
            return b * nb + jnp.where(i == 0, nb - 1, i - 1)

    in_specs = [
        pl.BlockSpec((GLA_BLOCK, kb), lambda b, i: (blk(b, i), 0)),
        pl.BlockSpec((GLA_BLOCK, kb), lambda b, i: (blk(b, i), 1)),
        pl.BlockSpec((GLA_BLOCK, GLA_VALUE_DIM), lambda b, i: (blk(b, i), 1)),
        pl.BlockSpec((GLA_BLOCK, LANES), lambda b, i: (blk(b, i), 0)),
        pl.BlockSpec((GLA_GATE_RANK, kb), lambda b, i: (0, 0)),
        pl.BlockSpec((1, kb), lambda b, i: (0, 0)),
    ]
    args = [proj, proj, proj, lr, up, bias.reshape(1, kb)]
    if backward:
        in_specs += [
            pl.BlockSpec((GLA_BLOCK, GLA_VALUE_DIM), lambda b, i: (blk(b, i), 0)),
            pl.BlockSpec((GLA_BLOCK, GLA_VALUE_DIM), lambda b, i: (blk(b, i), 2)),
            pl.BlockSpec((1, GLA_HEAD_V), lambda b, i: (0, 0)),
        ]
        args += [o_fwd, proj, o_norm.reshape(1, GLA_HEAD_V)]
    return pl.pallas_call(
        functools.partial(_gla_kernel, backward, nb),
        grid=(batch, nb),
        in_specs=in_specs,
        out_specs=pl.BlockSpec((GLA_BLOCK, GLA_VALUE_DIM), lambda b, i: (blk(b, i), 0)),
        out_shape=jax.ShapeDtypeStruct((np_, GLA_VALUE_DIM), BF16 if backward else F32),
        scratch_shapes=[pltpu.VMEM((GLA_HEADS, GLA_HEAD_V, GLA_HEAD_K), F32)],
        compiler_params=pltpu.CompilerParams(
            dimension_semantics=("parallel", "arbitrary"), vmem_limit_bytes=VMEM_LIMIT),
        name="gla_bwd" if backward else "gla_fwd",
    )(*args)


def _rope_perm():
    quarter = HEAD_DIM // 4
    idx = jnp.arange(HEAD_DIM).reshape(2, 2, quarter)
    return idx.transpose(1, 0, 2).reshape(HEAD_DIM)


def _rope_tables(s):
    rows = s // GRID_W
    row = jnp.repeat(jnp.arange(rows), GRID_W).astype(F32)
    col = jnp.tile(jnp.arange(GRID_W), rows).astype(F32)
    quarter = HEAD_DIM // 4
    inv_freq = ROPE_THETA ** (-jnp.arange(0, 2 * quarter, 2, dtype=F32) / (2 * quarter))
    ang_row = row[:, None] * inv_freq[None]
    ang_col = col[:, None] * inv_freq[None]
    ang = jnp.concatenate([ang_row, ang_col, ang_row, ang_col], axis=-1)
    ang = jnp.concatenate([ang, jnp.zeros((ROW_PAD, HEAD_DIM), F32)], axis=0)
    cos, sin = jnp.cos(ang), jnp.sin(ang)
    first_half = jnp.arange(HEAD_DIM) < HEAD_DIM // 2
    return cos, jnp.where(first_half[None], -sin, sin)


def kernel(x, meta_tokens, pre_norm, post_norm, attn_w_in, attn_q_norm, attn_k_norm,
           attn_w_out, gla_w_in, gla_gk_up, gla_gk_bias, gla_o_norm, gla_w_out):
    batch, s, d = x.shape
    assert d == D_MODEL and s % GLA_BLOCK == 0 and s % KV_TILE == 0 and s % GRID_W == 0
    lp = s + ROW_PAD
    depth = pre_norm.shape[0]

    meta = jnp.broadcast_to(meta_tokens.astype(x.dtype)[None], (batch, N_META, d))
    zeros = jnp.zeros((batch, ROW_PAD - N_META, d), x.dtype)
    h = jnp.concatenate([x, meta, zeros], axis=1).reshape(batch * lp, d)
    tables = _rope_tables(s)

    for i in range(depth):
        j = i // 2
        if i % 2 == 0:
            perm = _rope_perm()
            head_w = jnp.concatenate([
                jnp.tile(attn_q_norm[j][perm] * (HEAD_DIM ** -0.5 * LOG2_E), ATTN_HEADS),
                jnp.tile(attn_k_norm[j][perm], ATTN_KV_HEADS)]).reshape(1, -1)
            n_qk = ATTN_WIDTH + ATTN_KV_WIDTH
            w = attn_w_in[j]
            w_qk = w[:, :n_qk].reshape(d, n_qk // HEAD_DIM, HEAD_DIM)[:, :, perm].reshape(d, n_qk)
            w = jnp.concatenate([w_qk, w[:, n_qk:]], axis=1).astype(BF16)
            proj = _inproj(h, pre_norm[i], w, ATTN_IN, lp=lp, rope=(head_w,) + tables)
            q_gain = jnp.max(jnp.abs(head_w[0, :ATTN_WIDTH]))
            k_gain = jnp.max(jnp.abs(head_w[0, ATTN_WIDTH:]))
            bounded = (HEAD_DIM * q_gain * k_gain <= SCORE_BOUND).astype(jnp.int32).reshape(1)
            o = _attention(proj, bounded, batch=batch, lp=lp)
            h = _outproj(o, attn_w_out[j].astype(BF16), h, post_norm[i], lp=lp,
                         gate_src=proj, gate_col_block=(ATTN_IN - ATTN_WIDTH) // ATTN_WIDTH)
        else:
            col_scale = jnp.where(jnp.arange(GLA_IN) < GLA_KEY_DIM, GLA_HEAD_K ** -0.5, 1.0)
            w = (gla_w_in[j] * col_scale).astype(BF16)
            w_lr = jnp.pad(w[:, GLA_MAIN:], ((0, 0), (0, LANES - 2 * GLA_GATE_RANK)))
            proj, lr = _inproj(h, pre_norm[i], w, GLA_MAIN, lp=lp, w_lr=w_lr)
            o_f = _gla_direction(proj, lr, gla_gk_up[j, 0], gla_gk_bias[j, 0],
                                 batch=batch, lp=lp, backward=False)
            gated = _gla_direction(proj, lr, gla_gk_up[j, 1], gla_gk_bias[j, 1],
                                   batch=batch, lp=lp, backward=True,
                                   o_fwd=o_f, o_norm=gla_o_norm[j])
            h = _outproj(gated, gla_w_out[j].astype(BF16), h, post_norm[i], lp=lp)
    return h.reshape(batch, lp, d)[:, :s]
```

```python
import functools

import jax
import jax.numpy as jnp
from jax import lax
from jax.experimental import pallas as pl
from jax.experimental.pallas import tpu as pltpu

F32 = jnp.float32
BF16 = jnp.bfloat16

D_MODEL = 2048
N_META = 16
GRID_W = 64
NORM_EPS = 1e-6
ROPE_THETA = 10000.0
LOG2_E = 1.4426950408889634

HEAD_DIM = 128
ATTN_HEADS = 16
ATTN_KV_HEADS = 8
ATTN_GROUP = ATTN_HEADS // ATTN_KV_HEADS
ATTN_WIDTH = ATTN_HEADS * HEAD_DIM
ATTN_KV_WIDTH = ATTN_KV_HEADS * HEAD_DIM
ATTN_IN = 2 * ATTN_WIDTH + 2 * ATTN_KV_WIDTH

GLA_HEADS = 4
GLA_KEY_DIM = D_MODEL // 2
GLA_VALUE_DIM = D_MODEL
GLA_HEAD_K = GLA_KEY_DIM // GLA_HEADS
GLA_HEAD_V = GLA_VALUE_DIM // GLA_HEADS
GLA_GATE_RANK = 16
GLA_GATE_NORMALIZER = 16.0
GLA_CHUNK = 64
GLA_MAIN = 2 * GLA_KEY_DIM + 2 * GLA_VALUE_DIM
GLA_IN = GLA_MAIN + 2 * GLA_GATE_RANK

LANES = 128
ROW_PAD = 128
GLA_BLOCK = 128
VMEM_LIMIT = 56 * 1024 * 1024

PROJ_TN = 1024
KV_TILE = 1024
SCORE_BOUND = 60.0


def _pick_tile(n, candidates):
    for c in candidates:
        if n % c == 0:
            return c
    raise ValueError(f"no tile in {candidates} divides {n}")


def _split_hi_lo(a):
    hi = a.astype(BF16)
    lo = (a - hi.astype(F32)).astype(BF16)
    return hi, lo


def _inproj_kernel(n_rope_tiles, has_lr, tn, *refs):
    it = iter(refs)
    h_ref, pw_ref, w_ref = next(it), next(it), next(it)
    wlr_ref = next(it) if has_lr else None
    if n_rope_tiles:
        hn_ref, c_ref, sn_ref = next(it), next(it), next(it)
    o_ref = next(it)
    lr_ref = next(it) if has_lr else None
    xn_sc = next(it)

    j = pl.program_id(1)

    @pl.when(j == 0)
    def _():
        x = h_ref[...]
        ms = jnp.mean(x * x, axis=-1, keepdims=True)
        xn = (x * lax.rsqrt(ms + NORM_EPS) * pw_ref[...]).astype(BF16)
        xn_sc[...] = xn
        if has_lr:
            lr_ref[...] = jnp.dot(xn, wlr_ref[...], preferred_element_type=F32)

    acc = jnp.dot(xn_sc[...], w_ref[...], preferred_element_type=F32)

    if n_rope_tiles:
        @pl.when(j < n_rope_tiles)
        def _():
            c, sn = c_ref[...], sn_ref[...]
            for hh in range(tn // HEAD_DIM):
                cols = slice(hh * HEAD_DIM, (hh + 1) * HEAD_DIM)
                a = acc[:, cols]
                r = lax.rsqrt(jnp.mean(a * a, axis=-1, keepdims=True) + NORM_EPS)
                y = a * r * hn_ref[:, cols]
                y = y * c + pltpu.roll(y, HEAD_DIM // 2, 1) * sn
                o_ref[:, cols] = y.astype(BF16)

        @pl.when(j >= n_rope_tiles)
        def _():
            o_ref[...] = acc.astype(BF16)
    else:
        o_ref[...] = acc.astype(BF16)


def _inproj(h, pre_w, w, layer, n, *, lp, w_lr=None, rope=None):
    np_, d = h.shape
    tm = _pick_tile(lp, (704, 384, 128))
    tn = PROJ_TN
    tiles_per_batch = lp // tm
    has_lr = w_lr is not None
    n_rope_tiles = 0
    in_specs = [
        pl.BlockSpec((tm, d), lambda i, j: (i, 0)),
        pl.BlockSpec((1, d), lambda i, j: (0, 0)),
        pl.BlockSpec((None, d, tn), lambda i, j: (layer, 0, j)),
    ]
    args = [h, pre_w.reshape(1, d), w]
    if has_lr:
        in_specs.append(pl.BlockSpec((None, d, LANES), lambda i, j: (layer, 0, 0)))
        args.append(w_lr)
    if rope is not None:
        head_w, cos_t, sin_t = rope
        n_rope_tiles = head_w.shape[1] // tn
        last = n_rope_tiles - 1
        in_specs.append(pl.BlockSpec((1, tn), lambda i, j: (0, jnp.minimum(j, last))))
        tab = pl.BlockSpec((tm, HEAD_DIM), lambda i, j: (i % tiles_per_batch, 0))
        in_specs += [tab, tab]
        args += [head_w, cos_t, sin_t]
    out_shape = [jax.ShapeDtypeStruct((np_, n), BF16)]
    out_specs = [pl.BlockSpec((tm, tn), lambda i, j: (i, j))]
    if has_lr:
        out_shape.append(jax.ShapeDtypeStruct((np_, LANES), F32))
        out_specs.append(pl.BlockSpec((tm, LANES), lambda i, j: (i, 0)))
    res = pl.pallas_call(
        functools.partial(_inproj_kernel, n_rope_tiles, has_lr, tn),
        grid=(np_ // tm, n // tn),
        in_specs=in_specs,
        out_specs=out_specs,
        out_shape=out_shape,
        scratch_shapes=[pltpu.VMEM((tm, d), BF16)],
        compiler_params=pltpu.CompilerParams(
            dimension_semantics=("parallel", "arbitrary"), vmem_limit_bytes=VMEM_LIMIT),
        name="inproj_rope" if rope is not None else "inproj_gla",
    )(*args)
    return res if has_lr else res[0]


def _attn_kernel(s_real, tq, bounded_ref, q_ref, k_ref, v_ref, o_ref, vt_sc, m_sc, l_sc, acc_sc,
                 s_a, s_b, cm_a, cm_b):
    lp = s_real + ROW_PAD
    s_bufs, cm_bufs = (s_a, s_b), (cm_a, cm_b)

    @pl.when(pl.program_id(2) == 0)
    def _():
        for c in range(lp // LANES):
            cols = slice(c * LANES, (c + 1) * LANES)
            vt_sc[:, cols] = v_ref[cols, :].astype(F32).T.astype(BF16)

    q = jnp.concatenate([q_ref[:, g * HEAD_DIM:(g + 1) * HEAD_DIM] for g in range(ATTN_GROUP)],
                        axis=0)
    n_full = s_real // KV_TILE
    chunks = [(c * KV_TILE, KV_TILE, None) for c in range(n_full - 1)]
    chunks.append(((n_full - 1) * KV_TILE, KV_TILE + ROW_PAD, KV_TILE + N_META))

    def key_scores(idx):
        r0, rows, _ = chunks[idx]
        return lax.dot_general(k_ref[r0:r0 + rows, :], q, (((1,), (1,)), ((), ())),
                               preferred_element_type=F32)

    def mask_tail(idx, x, fill):
        _, rows, n_valid = chunks[idx]
        if n_valid is None:
            return x
        row = lax.broadcasted_iota(jnp.int32, (ROW_PAD, 1), 0) + (rows - ROW_PAD)
        tail = jnp.where(row < n_valid, x[rows - ROW_PAD:], fill)
        return jnp.concatenate([x[:rows - ROW_PAD], tail], axis=0)

    @pl.when(bounded_ref[0] != 0)
    def _():
        acc = jnp.zeros(acc_sc.shape, F32)
        l = jnp.zeros(l_sc.shape, F32)
        for idx, (r0, rows, _) in enumerate(chunks):
            p = mask_tail(idx, jnp.exp2(key_scores(idx)), 0.0)
            l = l + jnp.sum(p, axis=0, keepdims=True)
            acc = acc + jnp.dot(vt_sc[:, r0:r0 + rows], p.astype(BF16),
                                preferred_element_type=F32)
        acc_sc[...] = acc
        l_sc[...] = l

    @pl.when(bounded_ref[0] == 0)
    def _():
        m_sc[...] = jnp.full_like(m_sc, -jnp.inf)
        l_sc[...] = jnp.zeros_like(l_sc)
        acc_sc[...] = jnp.zeros_like(acc_sc)

        def scores(idx):
            rows = chunks[idx][1]
            s = mask_tail(idx, key_scores(idx), -jnp.inf)
            s_bufs[idx % 2][0:rows, :] = s
            cm_bufs[idx % 2][...] = jnp.max(s, axis=0, keepdims=True)

        def accumulate(idx):
            r0, rows, _ = chunks[idx]
            m_prev = m_sc[...]
            m_new = jnp.maximum(m_prev, cm_bufs[idx % 2][...])
            alpha = jnp.exp2(m_prev - m_new)
            p = jnp.exp2(s_bufs[idx % 2][0:rows, :] - m_new)
            l_sc[...] = alpha * l_sc[...] + jnp.sum(p, axis=0, keepdims=True)
            acc_sc[...] = alpha * acc_sc[...] + jnp.dot(
                vt_sc[:, r0:r0 + rows], p.astype(BF16), preferred_element_type=F32)
            m_sc[...] = m_new

        scores(0)
        for c in range(len(chunks)):
            if c + 1 < len(chunks):
                scores(c + 1)
            accumulate(c)

    o = (acc_sc[...] / l_sc[...]).T
    for g in range(ATTN_GROUP):
        o_ref[:, g * HEAD_DIM:(g + 1) * HEAD_DIM] = o[g * tq:(g + 1) * tq].astype(BF16)


def _attention(proj, bounded, *, batch, lp):
    np_ = proj.shape[0]
    s_real = lp - ROW_PAD
    tq = _pick_tile(lp, (384, 128))
    nq = lp // tq
    gw = ATTN_GROUP * HEAD_DIM
    k_col0 = ATTN_WIDTH // HEAD_DIM
    v_col0 = (ATTN_WIDTH + ATTN_KV_WIDTH) // HEAD_DIM
    return pl.pallas_call(
        functools.partial(_attn_kernel, s_real, tq),
        grid=(batch, ATTN_KV_HEADS, nq),
        in_specs=[
            pl.BlockSpec(memory_space=pltpu.SMEM),
            pl.BlockSpec((tq, gw), lambda b, h, i: (b * nq + i, h)),
            pl.BlockSpec((lp, HEAD_DIM), lambda b, h, i: (b, k_col0 + h)),
            pl.BlockSpec((lp, HEAD_DIM), lambda b, h, i: (b, v_col0 + h)),
        ],
        out_specs=pl.BlockSpec((tq, gw), lambda b, h, i: (b * nq + i, h)),
        out_shape=jax.ShapeDtypeStruct((np_, ATTN_WIDTH), BF16),
        scratch_shapes=[
            pltpu.VMEM((HEAD_DIM, lp), BF16),
            pltpu.VMEM((1, ATTN_GROUP * tq), F32),
            pltpu.VMEM((1, ATTN_GROUP * tq), F32),
            pltpu.VMEM((HEAD_DIM, ATTN_GROUP * tq), F32),
            pltpu.VMEM((KV_TILE + ROW_PAD, ATTN_GROUP * tq), F32),
            pltpu.VMEM((KV_TILE + ROW_PAD, ATTN_GROUP * tq), F32),
            pltpu.VMEM((1, ATTN_GROUP * tq), F32),
            pltpu.VMEM((1, ATTN_GROUP * tq), F32),
        ],
        compiler_params=pltpu.CompilerParams(
            dimension_semantics=("parallel", "parallel", "arbitrary"), vmem_limit_bytes=VMEM_LIMIT),
        name="attention",
    )(bounded, proj, proj, proj)


def _outproj_kernel(has_gate, *refs):
    it = iter(refs)
    x_ref = next(it)
    g_ref = next(it) if has_gate else None
    w_ref, h_ref, pw_ref, out_ref = next(it), next(it), next(it), next(it)
    x = x_ref[...]
    if has_gate:
        g = g_ref[...].astype(F32)
        x = (x.astype(F32) * (g * jax.nn.sigmoid(g))).astype(BF16)
    y = jnp.dot(x, w_ref[...], preferred_element_type=F32)
    r = lax.rsqrt(jnp.mean(y * y, axis=-1, keepdims=True) + NORM_EPS)
    out_ref[...] = h_ref[...] + y * r * pw_ref[...]


def _outproj(x, w, layer, h, post_w, *, batch, lp, rows, gate_src=None, gate_col_block=None):
    kdim = x.shape[1]
    d = w.shape[2]
    tm = _pick_tile(rows, (512, 384, 128))
    has_gate = gate_src is not None
    in_specs = [pl.BlockSpec((None, tm, kdim), lambda b, i: (b, i, 0))]
    args = [x.reshape(batch, lp, kdim)]
    if has_gate:
        in_specs.append(pl.BlockSpec((None, tm, kdim), lambda b, i: (b, i, gate_col_block)))
        args.append(gate_src.reshape(batch, lp, gate_src.shape[1]))
    in_specs += [
        pl.BlockSpec((None, kdim, d), lambda b, i: (layer, 0, 0)),
        pl.BlockSpec((None, tm, d), lambda b, i: (b, i, 0)),
        pl.BlockSpec((1, d), lambda b, i: (0, 0)),
    ]
    args += [w, h.reshape(batch, lp, d), post_w.reshape(1, d)]
    out = pl.pallas_call(
        functools.partial(_outproj_kernel, has_gate),
        grid=(batch, rows // tm),
        in_specs=in_specs,
        out_specs=pl.BlockSpec((None, tm, d), lambda b, i: (b, i, 0)),
        out_shape=jax.ShapeDtypeStruct((batch, rows, d), F32),
        compiler_params=pltpu.CompilerParams(
            dimension_semantics=("parallel", "parallel"), vmem_limit_bytes=VMEM_LIMIT),
        name="outproj_gated" if has_gate else "outproj",
    )(*args)
    return out.reshape(batch * rows, d)


def _gla_kernel(backward, n_blocks, *refs):
    it = iter(refs)
    q_ref, k_ref, v_ref, lr_ref, up_ref, bias_ref = (next(it) for _ in range(6))
    if backward:
        of_ref, gate_ref, onorm_ref = next(it), next(it), next(it)
    out_ref, st_sc = next(it), next(it)

    i = pl.program_id(1)

    @pl.when(i == 0)
    def _():
        st_sc[...] = jnp.zeros_like(st_sc)

    is_meta = (i == n_blocks - 1) if backward else (i == 0)
    n_valid = jnp.where(is_meta, N_META, GLA_BLOCK)

    C, R = GLA_CHUNK, GLA_BLOCK
    row = lax.broadcasted_iota(jnp.int32, (R, R), 0)
    col = lax.broadcasted_iota(jnp.int32, (R, R), 1)
    same_chunk = (row // C) == (col // C)
    if backward:
        cum_mat = (same_chunk & (col >= row)).astype(BF16)
        own_mask = same_chunk & (col > row)
        cross_mask = (row < C) & (col >= C)
        e0, e1 = 0, C
    else:
        cum_mat = (same_chunk & (col <= row)).astype(BF16)
        own_mask = same_chunk & (col <= row)
        cross_mask = (row >= C) & (col < C)
        e0, e1 = C - 1, R - 1
    gate_cols = slice(GLA_GATE_RANK, 2 * GLA_GATE_RANK) if backward else slice(0, GLA_GATE_RANK)

    rows1 = lax.broadcasted_iota(jnp.int32, (R, 1), 0)
    valid = rows1 < n_valid
    first = rows1 < C

    z = jnp.dot(lr_ref[:, gate_cols].astype(BF16), up_ref[...].astype(BF16),
                preferred_element_type=F32) + bias_ref[...]
    log_sig = jnp.minimum(z, 0.0) - jnp.log1p(jnp.exp(-jnp.abs(z)))
    g = jnp.where(valid, log_sig / GLA_GATE_NORMALIZER, 0.0)
    g_hi, g_lo = _split_hi_lo(g)
    b = (jnp.dot(cum_mat, g_hi, preferred_element_type=F32)
         + jnp.dot(cum_mat, g_lo, preferred_element_type=F32))
    tot0, tot1 = b[e0:e0 + 1, :], b[e1:e1 + 1, :]
    tot_own = jnp.where(first, tot0, tot1)
    q = q_ref[...].astype(F32)
    k = jnp.where(valid, k_ref[...].astype(F32), 0.0)
    v_all = jnp.where(valid, v_ref[...], jnp.zeros((), BF16))
    qd = q * jnp.exp(b)
    ki = (k * jnp.exp(-b)).astype(BF16)
    ke = k * jnp.exp(tot_own - b)
    if backward:
        q_st = jnp.concatenate([qd[:C] * jnp.exp(tot1), qd[C:]], axis=0)
        ke_st = jnp.concatenate([ke[:C], ke[C:] * jnp.exp(tot0)], axis=0)
    else:
        q_st = jnp.concatenate([qd[:C], qd[C:] * jnp.exp(tot0)], axis=0)
        ke_st = jnp.concatenate([ke[:C] * jnp.exp(tot1), ke[C:]], axis=0)
    qd, ke = qd.astype(BF16), ke.astype(BF16)
    q_st, ke_st = q_st.astype(BF16), ke_st.astype(BF16)
    decay = jnp.exp(tot0 + tot1)

    for hd in range(GLA_HEADS):
        kc = slice(hd * GLA_HEAD_K, (hd + 1) * GLA_HEAD_K)
        vc = slice(hd * GLA_HEAD_V, (hd + 1) * GLA_HEAD_V)
        v = v_all[:, vc]
        keys = jnp.concatenate([ki[:, kc], ke[:, kc]], axis=0)
        p = lax.dot_general(qd[:, kc], keys, (((1,), (1,)), ((), ())),
                            preferred_element_type=F32)
        sc = (jnp.where(own_mask, p[:, :R], 0.0)
              + jnp.where(cross_mask, p[:, R:], 0.0)).astype(BF16)
        st = st_sc[hd]
        o = (jnp.dot(sc, v, preferred_element_type=F32)
             + lax.dot_general(q_st[:, kc], st.astype(BF16), (((1,), (1,)), ((), ())),
                               preferred_element_type=F32))
        st_sc[hd] = st * decay[:, kc] + lax.dot_general(
            v, ke_st[:, kc], (((0,), (0,)), ((), ())), preferred_element_type=F32)
        if backward:
            tot = o + of_ref[:, vc].astype(F32)
            r = lax.rsqrt(jnp.mean(tot * tot, axis=-1, keepdims=True) + NORM_EPS)
            gt = gate_ref[:, vc].astype(F32)
            out_ref[:, vc] = (tot * r * onorm_ref[...] * (gt * jax.nn.sigmoid(gt))).astype(BF16)
        else:
            out_ref[:, vc] = o.astype(BF16)


def _gla_direction(proj, lr, up, bias, *, batch, lp, backward, o_fwd=None, o_norm=None):
    np_ = proj.shape[0]
    nb = lp // GLA_BLOCK
    kb = GLA_KEY_DIM

    if backward:
        def blk(b, i):
            return b * nb + jnp.where(i == nb - 1, nb - 1, nb - 2 - i)
    else:
        def blk(b, i):
            return b * nb + jnp.where(i == 0, nb - 1, i - 1)

    in_specs = [
        pl.BlockSpec((GLA_BLOCK, kb), lambda b, i: (blk(b, i), 0)),
        pl.BlockSpec((GLA_BLOCK, kb), lambda b, i: (blk(b, i), 1)),
        pl.BlockSpec((GLA_BLOCK, GLA_VALUE_DIM), lambda b, i: (blk(b, i), 1)),
        pl.BlockSpec((GLA_BLOCK, LANES), lambda b, i: (blk(b, i), 0)),
        pl.BlockSpec((GLA_GATE_RANK, kb), lambda b, i: (0, 0)),
        pl.BlockSpec((1, kb), lambda b, i: (0, 0)),
    ]
    args = [proj, proj, proj, lr, up, bias.reshape(1, kb)]
    if backward:
        in_specs += [
            pl.BlockSpec((GLA_BLOCK, GLA_VALUE_DIM), lambda b, i: (blk(b, i), 0)),
            pl.BlockSpec((GLA_BLOCK, GLA_VALUE_DIM), lambda b, i: (blk(b, i), 2)),
            pl.BlockSpec((1, GLA_HEAD_V), lambda b, i: (0, 0)),
        ]
        args += [o_fwd, proj, o_norm.reshape(1, GLA_HEAD_V)]
    return pl.pallas_call(
        functools.partial(_gla_kernel, backward, nb),
        grid=(batch, nb),
        in_specs=in_specs,
        out_specs=pl.BlockSpec((GLA_BLOCK, GLA_VALUE_DIM), lambda b, i: (blk(b, i), 0)),
        out_shape=jax.ShapeDtypeStruct((np_, GLA_VALUE_DIM), BF16),
        scratch_shapes=[pltpu.VMEM((GLA_HEADS, GLA_HEAD_V, GLA_HEAD_K), F32)],
        compiler_params=pltpu.CompilerParams(
            dimension_semantics=("parallel", "arbitrary"), vmem_limit_bytes=VMEM_LIMIT),
        name="gla_bwd" if backward else "gla_fwd",
    )(*args)


def _rope_perm():
    quarter = HEAD_DIM // 4
    idx = jnp.arange(HEAD_DIM).reshape(2, 2, quarter)
    return idx.transpose(1, 0, 2).reshape(HEAD_DIM)


def _rope_tables(s):
    rows = s // GRID_W
    row = jnp.repeat(jnp.arange(rows), GRID_W).astype(F32)
    col = jnp.tile(jnp.arange(GRID_W), rows).astype(F32)
    quarter = HEAD_DIM // 4
    inv_freq = ROPE_THETA ** (-jnp.arange(0, 2 * quarter, 2, dtype=F32) / (2 * quarter))
    ang_row = row[:, None] * inv_freq[None]
    ang_col = col[:, None] * inv_freq[None]
    ang = jnp.concatenate([ang_row, ang_col, ang_row, ang_col], axis=-1)
    ang = jnp.concatenate([ang, jnp.zeros((ROW_PAD, HEAD_DIM), F32)], axis=0)
    cos, sin = jnp.cos(ang), jnp.sin(ang)
    first_half = jnp.arange(HEAD_DIM) < HEAD_DIM // 2
    return cos, jnp.where(first_half[None], -sin, sin)


def kernel(x, meta_tokens, pre_norm, post_norm, attn_w_in, attn_q_norm, attn_k_norm,
           attn_w_out, gla_w_in, gla_gk_up, gla_gk_bias, gla_o_norm, gla_w_out):
    batch, s, d = x.shape
    assert d == D_MODEL and s % GLA_BLOCK == 0 and s % KV_TILE == 0 and s % GRID_W == 0
    lp = s + ROW_PAD
    depth = pre_norm.shape[0]

    meta = jnp.broadcast_to(meta_tokens.astype(x.dtype)[None], (batch, N_META, d))
    zeros = jnp.zeros((batch, ROW_PAD - N_META, d), x.dtype)
    h = jnp.concatenate([x, meta, zeros], axis=1).reshape(batch * lp, d)
    tables = _rope_tables(s)

    perm = _rope_perm()
    n_qk = ATTN_WIDTH + ATTN_KV_WIDTH
    n_attn = attn_w_in.shape[0]
    w_qk = attn_w_in[:, :, :n_qk].reshape(n_attn, d, n_qk // HEAD_DIM, HEAD_DIM)[..., perm]
    attn_w = jnp.concatenate([w_qk.reshape(n_attn, d, n_qk), attn_w_in[:, :, n_qk:]],
                             axis=-1).astype(BF16)
    col_scale = jnp.where(jnp.arange(GLA_IN) < GLA_KEY_DIM, GLA_HEAD_K ** -0.5, 1.0)
    gla_w = (gla_w_in * col_scale).astype(BF16)
    gla_w_lr = jnp.pad(gla_w[:, :, GLA_MAIN:], ((0, 0), (0, 0), (0, LANES - 2 * GLA_GATE_RANK)))
    attn_wo = attn_w_out.astype(BF16)
    gla_wo = gla_w_out.astype(BF16)

    for i in range(depth):
        j = i // 2
        rows = s if i == depth - 1 else lp
        if i % 2 == 0:
            head_w = jnp.concatenate([
                jnp.tile(attn_q_norm[j][perm] * (HEAD_DIM ** -0.5 * LOG2_E), ATTN_HEADS),
                jnp.tile(attn_k_norm[j][perm], ATTN_KV_HEADS)]).reshape(1, -1)
            proj = _inproj(h, pre_norm[i], attn_w, j, ATTN_IN, lp=lp, rope=(head_w,) + tables)
            q_gain = jnp.max(jnp.abs(head_w[0, :ATTN_WIDTH]))
            k_gain = jnp.max(jnp.abs(head_w[0, ATTN_WIDTH:]))
            bounded = (HEAD_DIM * q_gain * k_gain <= SCORE_BOUND).astype(jnp.int32).reshape(1)
            o = _attention(proj, bounded, batch=batch, lp=lp)
            h = _outproj(o, attn_wo, j, h, post_norm[i], batch=batch, lp=lp, rows=rows,
                         gate_src=proj, gate_col_block=(ATTN_IN - ATTN_WIDTH) // ATTN_WIDTH)
        else:
            proj, lr = _inproj(h, pre_norm[i], gla_w, j, GLA_MAIN, lp=lp, w_lr=gla_w_lr)
            o_f = _gla_direction(proj, lr, gla_gk_up[j, 0], gla_gk_bias[j, 0],
                                 batch=batch, lp=lp, backward=False)
            gated = _gla_direction(proj, lr, gla_gk_up[j, 1], gla_gk_bias[j, 1],
                                   batch=batch, lp=lp, backward=True,
                                   o_fwd=o_f, o_norm=gla_o_norm[j])
            h = _outproj(gated, gla_wo, j, h, post_norm[i], batch=batch, lp=lp, rows=rows)
    return h.reshape(batch, s, d)
```

```python
import functools

import jax
import jax.numpy as jnp
from jax import lax
from jax.experimental import pallas as pl
from jax.experimental.pallas import tpu as pltpu

F32 = jnp.float32
BF16 = jnp.bfloat16

D_MODEL = 2048
N_META = 16
GRID_W = 64
NORM_EPS = 1e-6
ROPE_THETA = 10000.0
LOG2_E = 1.4426950408889634

HEAD_DIM = 128
ATTN_HEADS = 16
ATTN_KV_HEADS = 8
ATTN_GROUP = ATTN_HEADS // ATTN_KV_HEADS
ATTN_WIDTH = ATTN_HEADS * HEAD_DIM
ATTN_KV_WIDTH = ATTN_KV_HEADS * HEAD_DIM
ATTN_IN = 2 * ATTN_WIDTH + 2 * ATTN_KV_WIDTH

GLA_HEADS = 4
GLA_KEY_DIM = D_MODEL // 2
GLA_VALUE_DIM = D_MODEL
GLA_HEAD_K = GLA_KEY_DIM // GLA_HEADS
GLA_HEAD_V = GLA_VALUE_DIM // GLA_HEADS
GLA_GATE_RANK = 16
GLA_GATE_NORMALIZER = 16.0
GLA_CHUNK = 64
GLA_MAIN = 2 * GLA_KEY_DIM + 2 * GLA_VALUE_DIM
GLA_IN = GLA_MAIN + 2 * GLA_GATE_RANK

LANES = 128
ROW_PAD = 128
GLA_BLOCK = 128
VMEM_LIMIT = 56 * 1024 * 1024

PROJ_TN = 1024
WPREP_ROWS = 256
WPREP_ROWS_T = 512
KV_TILE = 1024
SCORE_BOUND = 60.0


def _pick_tile(n, candidates):
    for c in candidates:
        if n % c == 0:
            return c
    raise ValueError(f"no tile in {candidates} divides {n}")


def _split_hi_lo(a):
    hi = a.astype(BF16)
    lo = (a - hi.astype(F32)).astype(BF16)
    return hi, lo


def _inproj_kernel(n_rope_tiles, has_lr, tn, w_transposed, *refs):
    it = iter(refs)
    h_ref, pw_ref, w_ref = next(it), next(it), next(it)
    wlr_ref = next(it) if has_lr else None
    if n_rope_tiles:
        hn_ref, c_ref, sn_ref = next(it), next(it), next(it)
    o_ref = next(it)
    lr_ref = next(it) if has_lr else None
    xn_sc = next(it)

    j = pl.program_id(1)

    def project(x, w):
        contract = (((1,), (1,)), ((), ())) if w_transposed else (((1,), (0,)), ((), ()))
        return lax.dot_general(x, w, contract, preferred_element_type=F32)

    @pl.when(j == 0)
    def _():
        x = h_ref[...]
        ms = jnp.mean(x * x, axis=-1, keepdims=True)
        xn = (x * lax.rsqrt(ms + NORM_EPS) * pw_ref[...]).astype(BF16)
        xn_sc[...] = xn
        if has_lr:
            lr_ref[...] = project(xn, wlr_ref[...])

    acc = project(xn_sc[...], w_ref[...])

    if n_rope_tiles:
        @pl.when(j < n_rope_tiles)
        def _():
            c, sn = c_ref[...], sn_ref[...]
            for hh in range(tn // HEAD_DIM):
                cols = slice(hh * HEAD_DIM, (hh + 1) * HEAD_DIM)
                a = acc[:, cols]
                r = lax.rsqrt(jnp.mean(a * a, axis=-1, keepdims=True) + NORM_EPS)
                y = a * r * hn_ref[:, cols]
                y = y * c + pltpu.roll(y, HEAD_DIM // 2, 1) * sn
                o_ref[:, cols] = y.astype(BF16)

        @pl.when(j >= n_rope_tiles)
        def _():
            o_ref[...] = acc.astype(BF16)
    else:
        o_ref[...] = acc.astype(BF16)


def _inproj(h, pre_w, w, layer, n, *, lp, w_lr=None, rope=None, w_transposed=False):
    np_, d = h.shape
    tm = _pick_tile(lp, (704, 384, 128))
    tn = PROJ_TN
    tiles_per_batch = lp // tm
    has_lr = w_lr is not None
    n_rope_tiles = 0
    in_specs = [
        pl.BlockSpec((tm, d), lambda i, j: (i, 0)),
        pl.BlockSpec((1, d), lambda i, j: (0, 0)),
        (pl.BlockSpec((None, tn, d), lambda i, j: (layer, j, 0)) if w_transposed
         else pl.BlockSpec((None, d, tn), lambda i, j: (layer, 0, j))),
    ]
    args = [h, pre_w.reshape(1, d), w]
    if has_lr:
        in_specs.append(pl.BlockSpec((None, LANES, d), lambda i, j: (layer, 0, 0)) if w_transposed
                        else pl.BlockSpec((None, d, LANES), lambda i, j: (layer, 0, 0)))
        args.append(w_lr)
    if rope is not None:
        head_w, cos_t, sin_t = rope
        n_rope_tiles = head_w.shape[1] // tn
        last = n_rope_tiles - 1
        in_specs.append(pl.BlockSpec((1, tn), lambda i, j: (0, jnp.minimum(j, last))))
        tab = pl.BlockSpec((tm, HEAD_DIM), lambda i, j: (i % tiles_per_batch, 0))
        in_specs += [tab, tab]
        args += [head_w, cos_t, sin_t]
    out_shape = [jax.ShapeDtypeStruct((np_, n), BF16)]
    out_specs = [pl.BlockSpec((tm, tn), lambda i, j: (i, j))]
    if has_lr:
        out_shape.append(jax.ShapeDtypeStruct((np_, LANES), F32))
        out_specs.append(pl.BlockSpec((tm, LANES), lambda i, j: (i, 0)))
    res = pl.pallas_call(
        functools.partial(_inproj_kernel, n_rope_tiles, has_lr, tn, w_transposed),
        grid=(np_ // tm, n // tn),
        in_specs=in_specs,
        out_specs=out_specs,
        out_shape=out_shape,
        scratch_shapes=[pltpu.VMEM((tm, d), BF16)],
        compiler_params=pltpu.CompilerParams(
            dimension_semantics=("parallel", "arbitrary"), vmem_limit_bytes=VMEM_LIMIT),
        name="inproj_rope" if rope is not None else "inproj_gla",
    )(*args)
    return res if has_lr else res[0]


def _attn_kernel(s_real, tq, bounded_ref, q_ref, k_ref, v_ref, o_ref, vt_sc, m_sc, l_sc, acc_sc,
                 s_a, s_b, cm_a, cm_b):
    lp = s_real + ROW_PAD
    s_bufs, cm_bufs = (s_a, s_b), (cm_a, cm_b)

    @pl.when(pl.program_id(2) == 0)
    def _():
        for c in range(lp // LANES):
            cols = slice(c * LANES, (c + 1) * LANES)
            vt_sc[:, cols] = v_ref[cols, :].astype(F32).T.astype(BF16)

    q = jnp.concatenate([q_ref[:, g * HEAD_DIM:(g + 1) * HEAD_DIM] for g in range(ATTN_GROUP)],
                        axis=0)
    n_full = s_real // KV_TILE
    chunks = [(c * KV_TILE, KV_TILE, None) for c in range(n_full - 1)]
    chunks.append(((n_full - 1) * KV_TILE, KV_TILE + ROW_PAD, KV_TILE + N_META))

    def key_scores(idx):
        r0, rows, _ = chunks[idx]
        return lax.dot_general(k_ref[r0:r0 + rows, :], q, (((1,), (1,)), ((), ())),
                               preferred_element_type=F32)

    def mask_tail(idx, x, fill):
        _, rows, n_valid = chunks[idx]
        if n_valid is None:
            return x
        row = lax.broadcasted_iota(jnp.int32, (ROW_PAD, 1), 0) + (rows - ROW_PAD)
        tail = jnp.where(row < n_valid, x[rows - ROW_PAD:], fill)
        return jnp.concatenate([x[:rows - ROW_PAD], tail], axis=0)

    @pl.when(bounded_ref[0] != 0)
    def _():
        acc = jnp.zeros(acc_sc.shape, F32)
        l = jnp.zeros(l_sc.shape, F32)
        for idx, (r0, rows, _) in enumerate(chunks):
            p = mask_tail(idx, jnp.exp2(key_scores(idx)), 0.0)
            l = l + jnp.sum(p, axis=0, keepdims=True)
            acc = acc + jnp.dot(vt_sc[:, r0:r0 + rows], p.astype(BF16),
                                preferred_element_type=F32)
        acc_sc[...] = acc
        l_sc[...] = l

    @pl.when(bounded_ref[0] == 0)
    def _():
        m_sc[...] = jnp.full_like(m_sc, -jnp.inf)
        l_sc[...] = jnp.zeros_like(l_sc)
        acc_sc[...] = jnp.zeros_like(acc_sc)

        def scores(idx):
            rows = chunks[idx][1]
            s = mask_tail(idx, key_scores(idx), -jnp.inf)
            s_bufs[idx % 2][0:rows, :] = s
            cm_bufs[idx % 2][...] = jnp.max(s, axis=0, keepdims=True)

        def accumulate(idx):
            r0, rows, _ = chunks[idx]
            m_prev = m_sc[...]
            m_new = jnp.maximum(m_prev, cm_bufs[idx % 2][...])
            alpha = jnp.exp2(m_prev - m_new)
            p = jnp.exp2(s_bufs[idx % 2][0:rows, :] - m_new)
            l_sc[...] = alpha * l_sc[...] + jnp.sum(p, axis=0, keepdims=True)
            acc_sc[...] = alpha * acc_sc[...] + jnp.dot(
                vt_sc[:, r0:r0 + rows], p.astype(BF16), preferred_element_type=F32)
            m_sc[...] = m_new

        scores(0)
        for c in range(len(chunks)):
            if c + 1 < len(chunks):
                scores(c + 1)
            accumulate(c)

    o = (acc_sc[...] / l_sc[...]).T
    for g in range(ATTN_GROUP):
        o_ref[:, g * HEAD_DIM:(g + 1) * HEAD_DIM] = o[g * tq:(g + 1) * tq].astype(BF16)


def _attention(proj, bounded, *, batch, lp):
    np_ = proj.shape[0]
    s_real = lp - ROW_PAD
    tq = _pick_tile(lp, (384, 128))
    nq = lp // tq
    gw = ATTN_GROUP * HEAD_DIM
    k_col0 = ATTN_WIDTH // HEAD_DIM
    v_col0 = (ATTN_WIDTH + ATTN_KV_WIDTH) // HEAD_DIM
    return pl.pallas_call(
        functools.partial(_attn_kernel, s_real, tq),
        grid=(batch, ATTN_KV_HEADS, nq),
        in_specs=[
            pl.BlockSpec(memory_space=pltpu.SMEM),
            pl.BlockSpec((tq, gw), lambda b, h, i: (b * nq + i, h)),
            pl.BlockSpec((lp, HEAD_DIM), lambda b, h, i: (b, k_col0 + h)),
            pl.BlockSpec((lp, HEAD_DIM), lambda b, h, i: (b, v_col0 + h)),
        ],
        out_specs=pl.BlockSpec((tq, gw), lambda b, h, i: (b * nq + i, h)),
        out_shape=jax.ShapeDtypeStruct((np_, ATTN_WIDTH), BF16),
        scratch_shapes=[
            pltpu.VMEM((HEAD_DIM, lp), BF16),
            pltpu.VMEM((1, ATTN_GROUP * tq), F32),
            pltpu.VMEM((1, ATTN_GROUP * tq), F32),
            pltpu.VMEM((HEAD_DIM, ATTN_GROUP * tq), F32),
            pltpu.VMEM((KV_TILE + ROW_PAD, ATTN_GROUP * tq), F32),
            pltpu.VMEM((KV_TILE + ROW_PAD, ATTN_GROUP * tq), F32),
            pltpu.VMEM((1, ATTN_GROUP * tq), F32),
            pltpu.VMEM((1, ATTN_GROUP * tq), F32),
        ],
        compiler_params=pltpu.CompilerParams(
            dimension_semantics=("parallel", "parallel", "arbitrary"), vmem_limit_bytes=VMEM_LIMIT),
        name="attention",
    )(bounded, proj, proj, proj)


def _outproj_kernel(has_gate, *refs):
    it = iter(refs)
    x_ref = next(it)
    g_ref = next(it) if has_gate else None
    w_ref, h_ref, pw_ref, out_ref = next(it), next(it), next(it), next(it)
    x = x_ref[...]
    if has_gate:
        g = g_ref[...].astype(F32)
        x = (x.astype(F32) * (g * jax.nn.sigmoid(g))).astype(BF16)
    y = jnp.dot(x, w_ref[...], preferred_element_type=F32)
    r = lax.rsqrt(jnp.mean(y * y, axis=-1, keepdims=True) + NORM_EPS)
    out_ref[...] = h_ref[...] + y * r * pw_ref[...]


def _outproj(x, w, layer, h, post_w, *, batch, lp, rows, gate_src=None, gate_col_block=None):
    kdim = x.shape[1]
    d = w.shape[2]
    tm = _pick_tile(rows, (512, 384, 128))
    has_gate = gate_src is not None
    in_specs = [pl.BlockSpec((None, tm, kdim), lambda b, i: (b, i, 0))]
    args = [x.reshape(batch, lp, kdim)]
    if has_gate:
        in_specs.append(pl.BlockSpec((None, tm, kdim), lambda b, i: (b, i, gate_col_block)))
        args.append(gate_src.reshape(batch, lp, gate_src.shape[1]))
    in_specs += [
        pl.BlockSpec((None, kdim, d), lambda b, i: (layer, 0, 0)),
        pl.BlockSpec((None, tm, d), lambda b, i: (b, i, 0)),
        pl.BlockSpec((1, d), lambda b, i: (0, 0)),
    ]
    args += [w, h.reshape(batch, lp, d), post_w.reshape(1, d)]
    out = pl.pallas_call(
        functools.partial(_outproj_kernel, has_gate),
        grid=(batch, rows // tm),
        in_specs=in_specs,
        out_specs=pl.BlockSpec((None, tm, d), lambda b, i: (b, i, 0)),
        out_shape=jax.ShapeDtypeStruct((batch, rows, d), F32),
        compiler_params=pltpu.CompilerParams(
            dimension_semantics=("parallel", "parallel"), vmem_limit_bytes=VMEM_LIMIT),
        name="outproj_gated" if has_gate else "outproj",
    )(*args)
    return out.reshape(batch * rows, d)


def _gla_kernel(backward, n_blocks, *refs):
    it = iter(refs)
    q_ref, k_ref, v_ref, lr_ref, up_ref, bias_ref = (next(it) for _ in range(6))
    if backward:
        of_ref, gate_ref, onorm_ref = next(it), next(it), next(it)
    out_ref, st_sc = next(it), next(it)

    i = pl.program_id(1)

    @pl.when(i == 0)
    def _():
        st_sc[...] = jnp.zeros_like(st_sc)

    is_meta = (i == n_blocks - 1) if backward else (i == 0)
    n_valid = jnp.where(is_meta, N_META, GLA_BLOCK)

    C, R = GLA_CHUNK, GLA_BLOCK
    row = lax.broadcasted_iota(jnp.int32, (R, R), 0)
    col = lax.broadcasted_iota(jnp.int32, (R, R), 1)
    same_chunk = (row // C) == (col // C)
    if backward:
        cum_mat = (same_chunk & (col >= row)).astype(BF16)
        own_mask = same_chunk & (col > row)
        cross_mask = (row < C) & (col >= C)
        e0, e1 = 0, C
    else:
        cum_mat = (same_chunk & (col <= row)).astype(BF16)
        own_mask = same_chunk & (col <= row)
        cross_mask = (row >= C) & (col < C)
        e0, e1 = C - 1, R - 1
    gate_cols = slice(GLA_GATE_RANK, 2 * GLA_GATE_RANK) if backward else slice(0, GLA_GATE_RANK)

    rows1 = lax.broadcasted_iota(jnp.int32, (R, 1), 0)
    valid = rows1 < n_valid
    first = rows1 < C

    z = jnp.dot(lr_ref[:, gate_cols].astype(BF16), up_ref[...].astype(BF16),
                preferred_element_type=F32) + bias_ref[...]
    log_sig = jnp.minimum(z, 0.0) - jnp.log1p(jnp.exp(-jnp.abs(z)))
    g = jnp.where(valid, log_sig / GLA_GATE_NORMALIZER, 0.0)
    g_hi, g_lo = _split_hi_lo(g)
    b = (jnp.dot(cum_mat, g_hi, preferred_element_type=F32)
         + jnp.dot(cum_mat, g_lo, preferred_element_type=F32))
    tot0, tot1 = b[e0:e0 + 1, :], b[e1:e1 + 1, :]
    tot_own = jnp.where(first, tot0, tot1)
    q = q_ref[...].astype(F32)
    k = jnp.where(valid, k_ref[...].astype(F32), 0.0)
    v_all = jnp.where(valid, v_ref[...], jnp.zeros((), BF16))
    qd = q * jnp.exp(b)
    ki = (k * jnp.exp(-b)).astype(BF16)
    ke = k * jnp.exp(tot_own - b)
    if backward:
        q_st = jnp.concatenate([qd[:C] * jnp.exp(tot1), qd[C:]], axis=0)
        ke_st = jnp.concatenate([ke[:C], ke[C:] * jnp.exp(tot0)], axis=0)
    else:
        q_st = jnp.concatenate([qd[:C], qd[C:] * jnp.exp(tot0)], axis=0)
        ke_st = jnp.concatenate([ke[:C] * jnp.exp(tot1), ke[C:]], axis=0)
    qd, ke = qd.astype(BF16), ke.astype(BF16)
    q_st, ke_st = q_st.astype(BF16), ke_st.astype(BF16)
    decay = jnp.exp(tot0 + tot1)

    for hd in range(GLA_HEADS):
        kc = slice(hd * GLA_HEAD_K, (hd + 1) * GLA_HEAD_K)
        vc = slice(hd * GLA_HEAD_V, (hd + 1) * GLA_HEAD_V)
        v = v_all[:, vc]
        keys = jnp.concatenate([ki[:, kc], ke[:, kc]], axis=0)
        p = lax.dot_general(qd[:, kc], keys, (((1,), (1,)), ((), ())),
                            preferred_element_type=F32)
        sc = (jnp.where(own_mask, p[:, :R], 0.0)
              + jnp.where(cross_mask, p[:, R:], 0.0)).astype(BF16)
        st = st_sc[hd]
        o = (jnp.dot(sc, v, preferred_element_type=F32)
             + lax.dot_general(q_st[:, kc], st.astype(BF16), (((1,), (1,)), ((), ())),
                               preferred_element_type=F32))
        st_sc[hd] = st * decay[:, kc] + lax.dot_general(
            v, ke_st[:, kc], (((0,), (0,)), ((), ())), preferred_element_type=F32)
        if backward:
            tot = o + of_ref[:, vc].astype(F32)
            r = lax.rsqrt(jnp.mean(tot * tot, axis=-1, keepdims=True) + NORM_EPS)
            gt = gate_ref[:, vc].astype(F32)
            out_ref[:, vc] = (tot * r * onorm_ref[...] * (gt * jax.nn.sigmoid(gt))).astype(BF16)
        else:
            out_ref[:, vc] = o.astype(BF16)


def _gla_direction(proj, lr, up, bias, *, batch, lp, backward, o_fwd=None, o_norm=None):
    np_ = proj.shape[0]
    nb = lp // GLA_BLOCK
    kb = GLA_KEY_DIM

    if backward:
        def blk(b, i):
            return b * nb + jnp.where(i == nb - 1, nb - 1, nb - 2 - i)
    else:
        def blk(b, i):
            return b * nb + jnp.where(i == 0, nb - 1, i - 1)

    in_specs = [
        pl.BlockSpec((GLA_BLOCK, kb), lambda b, i: (blk(b, i), 0)),
        pl.BlockSpec((GLA_BLOCK, kb), lambda b, i: (blk(b, i), 1)),
        pl.BlockSpec((GLA_BLOCK, GLA_VALUE_DIM), lambda b, i: (blk(b, i), 1)),
        pl.BlockSpec((GLA_BLOCK, LANES), lambda b, i: (blk(b, i), 0)),
        pl.BlockSpec((GLA_GATE_RANK, kb), lambda b, i: (0, 0)),
        pl.BlockSpec((1, kb), lambda b, i: (0, 0)),
    ]
    args = [proj, proj, proj, lr, up, bias.reshape(1, kb)]
    if backward:
        in_specs += [
            pl.BlockSpec((GLA_BLOCK, GLA_VALUE_DIM), lambda b, i: (blk(b, i), 0)),
            pl.BlockSpec((GLA_BLOCK, GLA_VALUE_DIM), lambda b, i: (blk(b, i), 2)),
            pl.BlockSpec((1, GLA_HEAD_V), lambda b, i: (0, 0)),
        ]
        args += [o_fwd, proj, o_norm.reshape(1, GLA_HEAD_V)]
    return pl.pallas_call(
        functools.partial(_gla_kernel, backward, nb),
        grid=(batch, nb),
        in_specs=in_specs,
        out_specs=pl.BlockSpec((GLA_BLOCK, GLA_VALUE_DIM), lambda b, i: (blk(b, i), 0)),
        out_shape=jax.ShapeDtypeStruct((np_, GLA_VALUE_DIM), BF16),
        scratch_shapes=[pltpu.VMEM((GLA_HEADS, GLA_HEAD_V, GLA_HEAD_K), F32)],
        compiler_params=pltpu.CompilerParams(
            dimension_semantics=("parallel", "arbitrary"), vmem_limit_bytes=VMEM_LIMIT),
        name="gla_bwd" if backward else "gla_fwd",
    )(*args)


def _wprep_kernel(perm_cols, x_ref, o_ref):
    if perm_cols:
        group = (lax.broadcasted_iota(jnp.int32, (1, HEAD_DIM), 1) * 4) // HEAD_DIM
        for hh in range(perm_cols // HEAD_DIM):
            cols = slice(hh * HEAD_DIM, (hh + 1) * HEAD_DIM)
            xh = x_ref[:, cols]
            up32 = pltpu.roll(xh, HEAD_DIM - HEAD_DIM // 4, 1)
            down32 = pltpu.roll(xh, HEAD_DIM // 4, 1)
            o_ref[:, cols] = jnp.where(group == 1, up32,
                                       jnp.where(group == 2, down32, xh)).astype(BF16)
    o_ref[:, perm_cols:] = x_ref[:, perm_cols:].astype(BF16)


def _prep_weights(w, *, perm_cols=0):
    layers, k, n = w.shape
    tr = WPREP_ROWS
    return pl.pallas_call(
        functools.partial(_wprep_kernel, perm_cols),
        grid=(layers, k // tr),
        in_specs=[pl.BlockSpec((None, tr, n), lambda l, r: (l, r, 0))],
        out_specs=pl.BlockSpec((None, tr, n), lambda l, r: (l, r, 0)),
        out_shape=jax.ShapeDtypeStruct((layers, k, n), BF16),
        compiler_params=pltpu.CompilerParams(
            dimension_semantics=("parallel", "parallel"), vmem_limit_bytes=VMEM_LIMIT),
        name="weight_prep",
    )(w)


def _wprep_gla_kernel(scaled_tiles, scale, x_ref, tail_ref, o_ref, otail_ref):
    r = pl.program_id(1)
    o_ref[...] = (x_ref[...] * jnp.where(r < scaled_tiles, scale, 1.0)).astype(BF16)
    tail = tail_ref[...].astype(BF16)
    pad = jnp.zeros((LANES - tail.shape[0], tail.shape[1]), BF16)
    otail_ref[...] = jnp.concatenate([tail, pad], axis=0)


def _prep_gla_weights(w_t):
    layers, n_in, d = w_t.shape
    tail = n_in - GLA_MAIN
    tr = WPREP_ROWS_T
    return pl.pallas_call(
        functools.partial(_wprep_gla_kernel, GLA_KEY_DIM // tr, GLA_HEAD_K ** -0.5),
        grid=(layers, GLA_MAIN // tr),
        in_specs=[pl.BlockSpec((None, tr, d), lambda l, r: (l, r, 0)),
                  pl.BlockSpec((None, tail, d), lambda l, r: (l, GLA_MAIN // tail, 0))],
        out_specs=[pl.BlockSpec((None, tr, d), lambda l, r: (l, r, 0)),
                   pl.BlockSpec((None, LANES, d), lambda l, r: (l, 0, 0))],
        out_shape=[jax.ShapeDtypeStruct((layers, GLA_MAIN, d), BF16),
                   jax.ShapeDtypeStruct((layers, LANES, d), BF16)],
        compiler_params=pltpu.CompilerParams(
            dimension_semantics=("parallel", "arbitrary"), vmem_limit_bytes=VMEM_LIMIT),
        name="weight_prep_gla",
    )(w_t, w_t)


def _rope_perm():
    quarter = HEAD_DIM // 4
    idx = jnp.arange(HEAD_DIM).reshape(2, 2, quarter)
    return idx.transpose(1, 0, 2).reshape(HEAD_DIM)


def _rope_tables(s):
    rows = s // GRID_W
    row = jnp.repeat(jnp.arange(rows), GRID_W).astype(F32)
    col = jnp.tile(jnp.arange(GRID_W), rows).astype(F32)
    quarter = HEAD_DIM // 4
    inv_freq = ROPE_THETA ** (-jnp.arange(0, 2 * quarter, 2, dtype=F32) / (2 * quarter))
    ang_row = row[:, None] * inv_freq[None]
    ang_col = col[:, None] * inv_freq[None]
    ang = jnp.concatenate([ang_row, ang_col, ang_row, ang_col], axis=-1)
    ang = jnp.concatenate([ang, jnp.zeros((ROW_PAD, HEAD_DIM), F32)], axis=0)
    cos, sin = jnp.cos(ang), jnp.sin(ang)
    first_half = jnp.arange(HEAD_DIM) < HEAD_DIM // 2
    return cos, jnp.where(first_half[None], -sin, sin)


def kernel(x, meta_tokens, pre_norm, post_norm, attn_w_in, attn_q_norm, attn_k_norm,
           attn_w_out, gla_w_in, gla_gk_up, gla_gk_bias, gla_o_norm, gla_w_out):
    batch, s, d = x.shape
    assert d == D_MODEL and s % GLA_BLOCK == 0 and s % KV_TILE == 0 and s % GRID_W == 0
    lp = s + ROW_PAD
    depth = pre_norm.shape[0]

    meta = jnp.broadcast_to(meta_tokens.astype(x.dtype)[None], (batch, N_META, d))
    zeros = jnp.zeros((batch, ROW_PAD - N_META, d), x.dtype)
    h = jnp.concatenate([x, meta, zeros], axis=1).reshape(batch * lp, d)
    tables = _rope_tables(s)

    perm = _rope_perm()
    attn_w = _prep_weights(attn_w_in, perm_cols=ATTN_WIDTH + ATTN_KV_WIDTH)
    gla_w, gla_w_lr = _prep_gla_weights(jnp.swapaxes(gla_w_in, 1, 2))
    attn_wo = _prep_weights(attn_w_out)
    gla_wo = _prep_weights(gla_w_out)

    for i in range(depth):
        j = i // 2
        rows = s if i == depth - 1 else lp
        if i % 2 == 0:
            head_w = jnp.concatenate([
                jnp.tile(attn_q_norm[j][perm] * (HEAD_DIM ** -0.5 * LOG2_E), ATTN_HEADS),
                jnp.tile(attn_k_norm[j][perm], ATTN_KV_HEADS)]).reshape(1, -1)
            proj = _inproj(h, pre_norm[i], attn_w, j, ATTN_IN, lp=lp, rope=(head_w,) + tables)
            q_gain = jnp.max(jnp.abs(head_w[0, :ATTN_WIDTH]))
            k_gain = jnp.max(jnp.abs(head_w[0, ATTN_WIDTH:]))
            bounded = (HEAD_DIM * q_gain * k_gain <= SCORE_BOUND).astype(jnp.int32).reshape(1)
            o = _attention(proj, bounded, batch=batch, lp=lp)
            h = _outproj(o, attn_wo, j, h, post_norm[i], batch=batch, lp=lp, rows=rows,
                         gate_src=proj, gate_col_block=(ATTN_IN - ATTN_WIDTH) // ATTN_WIDTH)
        else:
            proj, lr = _inproj(h, pre_norm[i], gla_w, j, GLA_MAIN, lp=lp, w_lr=gla_w_lr,
                               w_transposed=True)
            o_f = _gla_direction(proj, lr, gla_gk_up[j, 0], gla_gk_bias[j, 0],
                                 batch=batch, lp=lp, backward=False)
            gated = _gla_direction(proj, lr, gla_gk_up[j, 1], gla_gk_bias[j, 1],
                                   batch=batch, lp=lp, backward=True,
                                   o_fwd=o_f, o_norm=gla_o_norm[j])
            h = _outproj(gated, gla_wo, j, h, post_norm[i], batch=batch, lp=lp, rows=rows)
    return h.reshape(batch, s, d)
```

```python
import functools

import jax
import jax.numpy as jnp
from jax import lax
from jax.experimental import pallas as pl
from jax.experimental.pallas import tpu as pltpu

F32 = jnp.float32
BF16 = jnp.bfloat16

D_MODEL = 2048
N_META = 16
GRID_W = 64
NORM_EPS = 1e-6
ROPE_THETA = 10000.0
LOG2_E = 1.4426950408889634

HEAD_DIM = 128
ATTN_HEADS = 16
ATTN_KV_HEADS = 8
ATTN_GROUP = ATTN_HEADS // ATTN_KV_HEADS
ATTN_WIDTH = ATTN_HEADS * HEAD_DIM
ATTN_KV_WIDTH = ATTN_KV_HEADS * HEAD_DIM
ATTN_IN = 2 * ATTN_WIDTH + 2 * ATTN_KV_WIDTH

GLA_HEADS = 4
GLA_KEY_DIM = D_MODEL // 2
GLA_VALUE_DIM = D_MODEL
GLA_HEAD_K = GLA_KEY_DIM // GLA_HEADS
GLA_HEAD_V = GLA_VALUE_DIM // GLA_HEADS
GLA_GATE_RANK = 16
GLA_GATE_NORMALIZER = 16.0
GLA_CHUNK = 64
GLA_MAIN = 2 * GLA_KEY_DIM + 2 * GLA_VALUE_DIM
GLA_IN = GLA_MAIN + 2 * GLA_GATE_RANK

LANES = 128
ROW_PAD = 128
GLA_BLOCK = 128
VMEM_LIMIT = 56 * 1024 * 1024

PROJ_TN = 1024
NORM_CHUNK = 128
WPREP_ROWS = 256
WPREP_ROWS_T = 512
KV_TILE = 1024
SCORE_BOUND = 60.0


def _pick_tile(n, candidates):
    for c in candidates:
        if n % c == 0:
            return c
    raise ValueError(f"no tile in {candidates} divides {n}")


def _split_hi_lo(a):
    hi = a.astype(BF16)
    lo = (a - hi.astype(F32)).astype(BF16)
    return hi, lo


def _inproj_kernel(n_i, n_j, n_rope_tiles, has_lr, tm, tn, w_transposed, *refs):
    it = iter(refs)
    h_ref, pw_ref, w_ref = next(it), next(it), next(it)
    wlr_ref = next(it) if has_lr else None
    if n_rope_tiles:
        hn_ref, c_ref, sn_ref = next(it), next(it), next(it)
    o_ref = next(it)
    lr_ref = next(it) if has_lr else None
    xn_sc, xn_next_sc, acc_sc = next(it), next(it), next(it)

    t = pl.program_id(0)
    u = t - n_j
    n_tiles = n_i * n_j

    def project(x, w):
        contract = (((1,), (1,)), ((), ())) if w_transposed else (((1,), (0,)), ((), ()))
        return lax.dot_general(x, w, contract, preferred_element_type=F32)

    def normalise_chunk():
        start = pl.multiple_of(jnp.minimum((t % n_j) * NORM_CHUNK, tm - NORM_CHUNK), 16)
        x = h_ref[pl.ds(start, NORM_CHUNK), :]
        ms = jnp.mean(x * x, axis=-1, keepdims=True)
        xn = (x * lax.rsqrt(ms + NORM_EPS) * pw_ref[...]).astype(BF16)
        xn_next_sc[pl.ds(start, NORM_CHUNK), :] = xn
        if has_lr:
            lr_ref[pl.ds(start, NORM_CHUNK), :] = project(xn, wlr_ref[...])

    def finish(acc, rope_tile):
        if not rope_tile:
            o_ref[...] = acc.astype(BF16)
            return
        c, sn = c_ref[...], sn_ref[...]
        for hh in range(tn // HEAD_DIM):
            cols = slice(hh * HEAD_DIM, (hh + 1) * HEAD_DIM)
            a = acc[:, cols]
            r = lax.rsqrt(jnp.mean(a * a, axis=-1, keepdims=True) + NORM_EPS)
            y = a * r * hn_ref[:, cols]
            y = y * c + pltpu.roll(y, HEAD_DIM // 2, 1) * sn
            o_ref[:, cols] = y.astype(BF16)

    def body(rope_tile):
        prev = acc_sc[...]
        acc_sc[...] = project(xn_sc[...], w_ref[...])
        finish(prev, rope_tile)
        normalise_chunk()

    @pl.when(t == 0)
    def _():
        acc_sc[...] = jnp.zeros_like(acc_sc)

    @pl.when(u < 0)
    def _():
        normalise_chunk()

    in_body = (u >= 0) & (u < n_tiles)

    @pl.when(in_body & (u % n_j == 0))
    def _():
        xn_sc[...] = xn_next_sc[...]

    prev_is_rope = (jnp.maximum(u - 1, 0) % n_j) < n_rope_tiles
    if n_rope_tiles:
        @pl.when(in_body & prev_is_rope)
        def _():
            body(True)

    @pl.when(in_body & jnp.logical_not(prev_is_rope))
    def _():
        body(False)

    @pl.when(u == n_tiles)
    def _():
        finish(acc_sc[...], False)


def _inproj(h, pre_w, w, layer, n, *, lp, w_lr=None, rope=None, w_transposed=False):
    np_, d = h.shape
    tm = _pick_tile(lp, (704, 384, 128))
    tn = PROJ_TN
    n_i, n_j = np_ // tm, n // tn
    n_tiles = n_i * n_j
    assert n_j * NORM_CHUNK >= tm and tm % 16 == 0
    tiles_per_batch = lp // tm
    has_lr = w_lr is not None
    n_rope_tiles = 0

    def tile(t):
        u = jnp.clip(t - n_j, 0, n_tiles - 1)
        return u // n_j, u % n_j

    def prev_tile(t):
        u = jnp.clip(t - n_j - 1, 0, n_tiles - 1)
        return u // n_j, u % n_j

    def norm_row(t):
        return jnp.minimum(t // n_j, n_i - 1)

    in_specs = [
        pl.BlockSpec((tm, d), lambda t: (norm_row(t), 0)),
        pl.BlockSpec((1, d), lambda t: (0, 0)),
        (pl.BlockSpec((None, tn, d), lambda t: (layer, tile(t)[1], 0)) if w_transposed
         else pl.BlockSpec((None, d, tn), lambda t: (layer, 0, tile(t)[1]))),
    ]
    args = [h, pre_w.reshape(1, d), w]
    if has_lr:
        in_specs.append(pl.BlockSpec((None, LANES, d), lambda t: (layer, 0, 0)) if w_transposed
                        else pl.BlockSpec((None, d, LANES), lambda t: (layer, 0, 0)))
        args.append(w_lr)
    if rope is not None:
        head_w, cos_t, sin_t = rope
        n_rope_tiles = head_w.shape[1] // tn
        last = n_rope_tiles - 1
        in_specs.append(pl.BlockSpec((1, tn), lambda t: (0, jnp.minimum(prev_tile(t)[1], last))))
        tab = pl.BlockSpec((tm, HEAD_DIM), lambda t: (prev_tile(t)[0] % tiles_per_batch, 0))
        in_specs += [tab, tab]
        args += [head_w, cos_t, sin_t]
    out_shape = [jax.ShapeDtypeStruct((np_, n), BF16)]
    out_specs = [pl.BlockSpec((tm, tn), lambda t: prev_tile(t))]
    if has_lr:
        out_shape.append(jax.ShapeDtypeStruct((np_, LANES), F32))
        out_specs.append(pl.BlockSpec((tm, LANES), lambda t: (norm_row(t), 0)))
    res = pl.pallas_call(
        functools.partial(_inproj_kernel, n_i, n_j, n_rope_tiles, has_lr, tm, tn, w_transposed),
        grid=(n_tiles + n_j + 1,),
        in_specs=in_specs,
        out_specs=out_specs,
        out_shape=out_shape,
        scratch_shapes=[pltpu.VMEM((tm, d), BF16), pltpu.VMEM((tm, d), BF16),
                        pltpu.VMEM((tm, tn), F32)],
        compiler_params=pltpu.CompilerParams(
            dimension_semantics=("arbitrary",), vmem_limit_bytes=VMEM_LIMIT),
        name="inproj_rope" if rope is not None else "inproj_gla",
    )(*args)
    return res if has_lr else res[0]


def _attn_kernel(s_real, tq, bounded_ref, q_ref, k_ref, v_ref, o_ref, vt_sc, m_sc, l_sc, acc_sc,
                 s_a, s_b, cm_a, cm_b):
    lp = s_real + ROW_PAD
    s_bufs, cm_bufs = (s_a, s_b), (cm_a, cm_b)

    @pl.when(pl.program_id(2) == 0)
    def _():
        for c in range(lp // LANES):
            cols = slice(c * LANES, (c + 1) * LANES)
            vt_sc[:, cols] = v_ref[cols, :].astype(F32).T.astype(BF16)

    q = jnp.concatenate([q_ref[:, g * HEAD_DIM:(g + 1) * HEAD_DIM] for g in range(ATTN_GROUP)],
                        axis=0)
    n_full = s_real // KV_TILE
    chunks = [(c * KV_TILE, KV_TILE, None) for c in range(n_full - 1)]
    chunks.append(((n_full - 1) * KV_TILE, KV_TILE + ROW_PAD, KV_TILE + N_META))

    def key_scores(idx):
        r0, rows, _ = chunks[idx]
        return lax.dot_general(k_ref[r0:r0 + rows, :], q, (((1,), (1,)), ((), ())),
                               preferred_element_type=F32)

    def mask_tail(idx, x, fill):
        _, rows, n_valid = chunks[idx]
        if n_valid is None:
            return x
        row = lax.broadcasted_iota(jnp.int32, (ROW_PAD, 1), 0) + (rows - ROW_PAD)
        tail = jnp.where(row < n_valid, x[rows - ROW_PAD:], fill)
        return jnp.concatenate([x[:rows - ROW_PAD], tail], axis=0)

    @pl.when(bounded_ref[0] != 0)
    def _():
        acc = jnp.zeros(acc_sc.shape, F32)
        l = jnp.zeros(l_sc.shape, F32)
        for idx, (r0, rows, _) in enumerate(chunks):
            p = mask_tail(idx, jnp.exp2(key_scores(idx)), 0.0)
            l = l + jnp.sum(p, axis=0, keepdims=True)
            acc = acc + jnp.dot(vt_sc[:, r0:r0 + rows], p.astype(BF16),
                                preferred_element_type=F32)
        acc_sc[...] = acc
        l_sc[...] = l

    @pl.when(bounded_ref[0] == 0)
    def _():
        m_sc[...] = jnp.full_like(m_sc, -jnp.inf)
        l_sc[...] = jnp.zeros_like(l_sc)
        acc_sc[...] = jnp.zeros_like(acc_sc)

        def scores(idx):
            rows = chunks[idx][1]
            s = mask_tail(idx, key_scores(idx), -jnp.inf)
            s_bufs[idx % 2][0:rows, :] = s
            cm_bufs[idx % 2][...] = jnp.max(s, axis=0, keepdims=True)

        def accumulate(idx):
            r0, rows, _ = chunks[idx]
            m_prev = m_sc[...]
            m_new = jnp.maximum(m_prev, cm_bufs[idx % 2][...])
            alpha = jnp.exp2(m_prev - m_new)
            p = jnp.exp2(s_bufs[idx % 2][0:rows, :] - m_new)
            l_sc[...] = alpha * l_sc[...] + jnp.sum(p, axis=0, keepdims=True)
            acc_sc[...] = alpha * acc_sc[...] + jnp.dot(
                vt_sc[:, r0:r0 + rows], p.astype(BF16), preferred_element_type=F32)
            m_sc[...] = m_new

        scores(0)
        for c in range(len(chunks)):
            if c + 1 < len(chunks):
                scores(c + 1)
            accumulate(c)

    o = (acc_sc[...] / l_sc[...]).T
    for g in range(ATTN_GROUP):
        o_ref[:, g * HEAD_DIM:(g + 1) * HEAD_DIM] = o[g * tq:(g + 1) * tq].astype(BF16)


def _attention(proj, bounded, *, batch, lp):
    np_ = proj.shape[0]
    s_real = lp - ROW_PAD
    tq = _pick_tile(lp, (384, 128))
    nq = lp // tq
    gw = ATTN_GROUP * HEAD_DIM
    k_col0 = ATTN_WIDTH // HEAD_DIM
    v_col0 = (ATTN_WIDTH + ATTN_KV_WIDTH) // HEAD_DIM
    return pl.pallas_call(
        functools.partial(_attn_kernel, s_real, tq),
        grid=(batch, ATTN_KV_HEADS, nq),
        in_specs=[
            pl.BlockSpec(memory_space=pltpu.SMEM),
            pl.BlockSpec((tq, gw), lambda b, h, i: (b * nq + i, h)),
            pl.BlockSpec((lp, HEAD_DIM), lambda b, h, i: (b, k_col0 + h)),
            pl.BlockSpec((lp, HEAD_DIM), lambda b, h, i: (b, v_col0 + h)),
        ],
        out_specs=pl.BlockSpec((tq, gw), lambda b, h, i: (b * nq + i, h)),
        out_shape=jax.ShapeDtypeStruct((np_, ATTN_WIDTH), BF16),
        scratch_shapes=[
            pltpu.VMEM((HEAD_DIM, lp), BF16),
            pltpu.VMEM((1, ATTN_GROUP * tq), F32),
            pltpu.VMEM((1, ATTN_GROUP * tq), F32),
            pltpu.VMEM((HEAD_DIM, ATTN_GROUP * tq), F32),
            pltpu.VMEM((KV_TILE + ROW_PAD, ATTN_GROUP * tq), F32),
            pltpu.VMEM((KV_TILE + ROW_PAD, ATTN_GROUP * tq), F32),
            pltpu.VMEM((1, ATTN_GROUP * tq), F32),
            pltpu.VMEM((1, ATTN_GROUP * tq), F32),
        ],
        compiler_params=pltpu.CompilerParams(
            dimension_semantics=("parallel", "parallel", "arbitrary"), vmem_limit_bytes=VMEM_LIMIT),
        name="attention",
    )(bounded, proj, proj, proj)


def _outproj_kernel(has_gate, *refs):
    it = iter(refs)
    x_ref = next(it)
    g_ref = next(it) if has_gate else None
    w_ref, h_ref, pw_ref, out_ref = next(it), next(it), next(it), next(it)
    x = x_ref[...]
    if has_gate:
        g = g_ref[...].astype(F32)
        x = (x.astype(F32) * (g * jax.nn.sigmoid(g))).astype(BF16)
    y = jnp.dot(x, w_ref[...], preferred_element_type=F32)
    r = lax.rsqrt(jnp.mean(y * y, axis=-1, keepdims=True) + NORM_EPS)
    out_ref[...] = h_ref[...] + y * r * pw_ref[...]


def _outproj(x, w, layer, h, post_w, *, batch, lp, rows, gate_src=None, gate_col_block=None):
    kdim = x.shape[1]
    d = w.shape[2]
    tm = _pick_tile(rows, (512, 384, 128))
    has_gate = gate_src is not None
    in_specs = [pl.BlockSpec((None, tm, kdim), lambda b, i: (b, i, 0))]
    args = [x.reshape(batch, lp, kdim)]
    if has_gate:
        in_specs.append(pl.BlockSpec((None, tm, kdim), lambda b, i: (b, i, gate_col_block)))
        args.append(gate_src.reshape(batch, lp, gate_src.shape[1]))
    in_specs += [
        pl.BlockSpec((None, kdim, d), lambda b, i: (layer, 0, 0)),
        pl.BlockSpec((None, tm, d), lambda b, i: (b, i, 0)),
        pl.BlockSpec((1, d), lambda b, i: (0, 0)),
    ]
    args += [w, h.reshape(batch, lp, d), post_w.reshape(1, d)]
    out = pl.pallas_call(
        functools.partial(_outproj_kernel, has_gate),
        grid=(batch, rows // tm),
        in_specs=in_specs,
        out_specs=pl.BlockSpec((None, tm, d), lambda b, i: (b, i, 0)),
        out_shape=jax.ShapeDtypeStruct((batch, rows, d), F32),
        compiler_params=pltpu.CompilerParams(
            dimension_semantics=("parallel", "parallel"), vmem_limit_bytes=VMEM_LIMIT),
        name="outproj_gated" if has_gate else "outproj",
    )(*args)
    return out.reshape(batch * rows, d)


def _gla_kernel(backward, n_blocks, *refs):
    it = iter(refs)
    q_ref, k_ref, v_ref, lr_ref, up_ref, bias_ref = (next(it) for _ in range(6))
    if backward:
        of_ref, gate_ref, onorm_ref = next(it), next(it), next(it)
    out_ref, st_sc = next(it), next(it)

    i = pl.program_id(1)

    @pl.when(i == 0)
    def _():
        st_sc[...] = jnp.zeros_like(st_sc)

    is_meta = (i == n_blocks - 1) if backward else (i == 0)
    n_valid = jnp.where(is_meta, N_META, GLA_BLOCK)

    C, R = GLA_CHUNK, GLA_BLOCK
    row = lax.broadcasted_iota(jnp.int32, (R, R), 0)
    col = lax.broadcasted_iota(jnp.int32, (R, R), 1)
    same_chunk = (row // C) == (col // C)
    if backward:
        cum_mat = (same_chunk & (col >= row)).astype(BF16)
        own_mask = same_chunk & (col > row)
        cross_mask = (row < C) & (col >= C)
        e0, e1 = 0, C
    else:
        cum_mat = (same_chunk & (col <= row)).astype(BF16)
        own_mask = same_chunk & (col <= row)
        cross_mask = (row >= C) & (col < C)
        e0, e1 = C - 1, R - 1
    gate_cols = slice(GLA_GATE_RANK, 2 * GLA_GATE_RANK) if backward else slice(0, GLA_GATE_RANK)

    rows1 = lax.broadcasted_iota(jnp.int32, (R, 1), 0)
    valid = rows1 < n_valid
    first = rows1 < C

    z = jnp.dot(lr_ref[:, gate_cols].astype(BF16), up_ref[...].astype(BF16),
                preferred_element_type=F32) + bias_ref[...]
    log_sig = jnp.minimum(z, 0.0) - jnp.log1p(jnp.exp(-jnp.abs(z)))
    g = jnp.where(valid, log_sig / GLA_GATE_NORMALIZER, 0.0)
    g_hi, g_lo = _split_hi_lo(g)
    b = (jnp.dot(cum_mat, g_hi, preferred_element_type=F32)
         + jnp.dot(cum_mat, g_lo, preferred_element_type=F32))
    tot0, tot1 = b[e0:e0 + 1, :], b[e1:e1 + 1, :]
    tot_own = jnp.where(first, tot0, tot1)
    q = q_ref[...].astype(F32)
    k = jnp.where(valid, k_ref[...].astype(F32), 0.0)
    v_all = jnp.where(valid, v_ref[...], jnp.zeros((), BF16))
    qd = q * jnp.exp(b)
    ki = (k * jnp.exp(-b)).astype(BF16)
    ke = k * jnp.exp(tot_own - b)
    if backward:
        q_st = jnp.concatenate([qd[:C] * jnp.exp(tot1), qd[C:]], axis=0)
        ke_st = jnp.concatenate([ke[:C], ke[C:] * jnp.exp(tot0)], axis=0)
    else:
        q_st = jnp.concatenate([qd[:C], qd[C:] * jnp.exp(tot0)], axis=0)
        ke_st = jnp.concatenate([ke[:C] * jnp.exp(tot1), ke[C:]], axis=0)
    qd, ke = qd.astype(BF16), ke.astype(BF16)
    q_st, ke_st = q_st.astype(BF16), ke_st.astype(BF16)
    decay = jnp.exp(tot0 + tot1)

    for hd in range(GLA_HEADS):
        kc = slice(hd * GLA_HEAD_K, (hd + 1) * GLA_HEAD_K)
        vc = slice(hd * GLA_HEAD_V, (hd + 1) * GLA_HEAD_V)
        v = v_all[:, vc]
        keys = jnp.concatenate([ki[:, kc], ke[:, kc]], axis=0)
        p = lax.dot_general(qd[:, kc], keys, (((1,), (1,)), ((), ())),
                            preferred_element_type=F32)
        sc = (jnp.where(own_mask, p[:, :R], 0.0)
              + jnp.where(cross_mask, p[:, R:], 0.0)).astype(BF16)
        st = st_sc[hd]
        o = (jnp.dot(sc, v, preferred_element_type=F32)
             + lax.dot_general(q_st[:, kc], st.astype(BF16), (((1,), (1,)), ((), ())),
                               preferred_element_type=F32))
        st_sc[hd] = st * decay[:, kc] + lax.dot_general(
            v, ke_st[:, kc], (((0,), (0,)), ((), ())), preferred_element_type=F32)
        if backward:
            tot = o + of_ref[:, vc].astype(F32)
            r = lax.rsqrt(jnp.mean(tot * tot, axis=-1, keepdims=True) + NORM_EPS)
            gt = gate_ref[:, vc].astype(F32)
            out_ref[:, vc] = (tot * r * onorm_ref[...] * (gt * jax.nn.sigmoid(gt))).astype(BF16)
        else:
            out_ref[:, vc] = o.astype(BF16)


def _gla_direction(proj, lr, up, bias, *, batch, lp, backward, o_fwd=None, o_norm=None):
    np_ = proj.shape[0]
    nb = lp // GLA_BLOCK
    kb = GLA_KEY_DIM

    if backward:
        def blk(b, i):
            return b * nb + jnp.where(i == nb - 1, nb - 1, nb - 2 - i)
    else:
        def blk(b, i):
            return b * nb + jnp.where(i == 0, nb - 1, i - 1)

    in_specs = [
        pl.BlockSpec((GLA_BLOCK, kb), lambda b, i: (blk(b, i), 0)),
        pl.BlockSpec((GLA_BLOCK, kb), lambda b, i: (blk(b, i), 1)),
        pl.BlockSpec((GLA_BLOCK, GLA_VALUE_DIM), lambda b, i: (blk(b, i), 1)),
        pl.BlockSpec((GLA_BLOCK, LANES), lambda b, i: (blk(b, i), 0)),
        pl.BlockSpec((GLA_GATE_RANK, kb), lambda b, i: (0, 0)),
        pl.BlockSpec((1, kb), lambda b, i: (0, 0)),
    ]
    args = [proj, proj, proj, lr, up, bias.reshape(1, kb)]
    if backward:
        in_specs += [
            pl.BlockSpec((GLA_BLOCK, GLA_VALUE_DIM), lambda b, i: (blk(b, i), 0)),
            pl.BlockSpec((GLA_BLOCK, GLA_VALUE_DIM), lambda b, i: (blk(b, i), 2)),
            pl.BlockSpec((1, GLA_HEAD_V), lambda b, i: (0, 0)),
        ]
        args += [o_fwd, proj, o_norm.reshape(1, GLA_HEAD_V)]
    return pl.pallas_call(
        functools.partial(_gla_kernel, backward, nb),
        grid=(batch, nb),
        in_specs=in_specs,
        out_specs=pl.BlockSpec((GLA_BLOCK, GLA_VALUE_DIM), lambda b, i: (blk(b, i), 0)),
        out_shape=jax.ShapeDtypeStruct((np_, GLA_VALUE_DIM), BF16),
        scratch_shapes=[pltpu.VMEM((GLA_HEADS, GLA_HEAD_V, GLA_HEAD_K), F32)],
        compiler_params=pltpu.CompilerParams(
            dimension_semantics=("parallel", "arbitrary"), vmem_limit_bytes=VMEM_LIMIT),
        name="gla_bwd" if backward else "gla_fwd",
    )(*args)


def _wprep_kernel(perm_cols, x_ref, o_ref):
    if perm_cols:
        group = (lax.broadcasted_iota(jnp.int32, (1, HEAD_DIM), 1) * 4) // HEAD_DIM
        for hh in range(perm_cols // HEAD_DIM):
            cols = slice(hh * HEAD_DIM, (hh + 1) * HEAD_DIM)
            xh = x_ref[:, cols]
            up32 = pltpu.roll(xh, HEAD_DIM - HEAD_DIM // 4, 1)
            down32 = pltpu.roll(xh, HEAD_DIM // 4, 1)
            o_ref[:, cols] = jnp.where(group == 1, up32,
                                       jnp.where(group == 2, down32, xh)).astype(BF16)
    o_ref[:, perm_cols:] = x_ref[:, perm_cols:].astype(BF16)


def _prep_weights(w, *, perm_cols=0):
    layers, k, n = w.shape
    tr = WPREP_ROWS
    return pl.pallas_call(
        functools.partial(_wprep_kernel, perm_cols),
        grid=(layers, k // tr),
        in_specs=[pl.BlockSpec((None, tr, n), lambda l, r: (l, r, 0))],
        out_specs=pl.BlockSpec((None, tr, n), lambda l, r: (l, r, 0)),
        out_shape=jax.ShapeDtypeStruct((layers, k, n), BF16),
        compiler_params=pltpu.CompilerParams(
            dimension_semantics=("parallel", "parallel"), vmem_limit_bytes=VMEM_LIMIT),
        name="weight_prep",
    )(w)


def _wprep_gla_kernel(scaled_tiles, scale, x_ref, tail_ref, o_ref, otail_ref):
    r = pl.program_id(1)
    o_ref[...] = (x_ref[...] * jnp.where(r < scaled_tiles, scale, 1.0)).astype(BF16)
    tail = tail_ref[...].astype(BF16)
    pad = jnp.zeros((LANES - tail.shape[0], tail.shape[1]), BF16)
    otail_ref[...] = jnp.concatenate([tail, pad], axis=0)


def _prep_gla_weights(w_t):
    layers, n_in, d = w_t.shape
    tail = n_in - GLA_MAIN
    tr = WPREP_ROWS_T
    return pl.pallas_call(
        functools.partial(_wprep_gla_kernel, GLA_KEY_DIM // tr, GLA_HEAD_K ** -0.5),
        grid=(layers, GLA_MAIN // tr),
        in_specs=[pl.BlockSpec((None, tr, d), lambda l, r: (l, r, 0)),
                  pl.BlockSpec((None, tail, d), lambda l, r: (l, GLA_MAIN // tail, 0))],
        out_specs=[pl.BlockSpec((None, tr, d), lambda l, r: (l, r, 0)),
                   pl.BlockSpec((None, LANES, d), lambda l, r: (l, 0, 0))],
        out_shape=[jax.ShapeDtypeStruct((layers, GLA_MAIN, d), BF16),
                   jax.ShapeDtypeStruct((layers, LANES, d), BF16)],
        compiler_params=pltpu.CompilerParams(
            dimension_semantics=("parallel", "arbitrary"), vmem_limit_bytes=VMEM_LIMIT),
        name="weight_prep_gla",
    )(w_t, w_t)


def _rope_perm():
    quarter = HEAD_DIM // 4
    idx = jnp.arange(HEAD_DIM).reshape(2, 2, quarter)
    return idx.transpose(1, 0, 2).reshape(HEAD_DIM)


def _rope_tables(s):
    rows = s // GRID_W
    row = jnp.repeat(jnp.arange(rows), GRID_W).astype(F32)
    col = jnp.tile(jnp.arange(GRID_W), rows).astype(F32)
    quarter = HEAD_DIM // 4
    inv_freq = ROPE_THETA ** (-jnp.arange(0, 2 * quarter, 2, dtype=F32) / (2 * quarter))
    ang_row = row[:, None] * inv_freq[None]
    ang_col = col[:, None] * inv_freq[None]
    ang = jnp.concatenate([ang_row, ang_col, ang_row, ang_col], axis=-1)
    ang = jnp.concatenate([ang, jnp.zeros((ROW_PAD, HEAD_DIM), F32)], axis=0)
    cos, sin = jnp.cos(ang), jnp.sin(ang)
    first_half = jnp.arange(HEAD_DIM) < HEAD_DIM // 2
    return cos, jnp.where(first_half[None], -sin, sin)


def kernel(x, meta_tokens, pre_norm, post_norm, attn_w_in, attn_q_norm, attn_k_norm,
           attn_w_out, gla_w_in, gla_gk_up, gla_gk_bias, gla_o_norm, gla_w_out):
    batch, s, d = x.shape
    assert d == D_MODEL and s % GLA_BLOCK == 0 and s % KV_TILE == 0 and s % GRID_W == 0
    lp = s + ROW_PAD
    depth = pre_norm.shape[0]

    meta = jnp.broadcast_to(meta_tokens.astype(x.dtype)[None], (batch, N_META, d))
    zeros = jnp.zeros((batch, ROW_PAD - N_META, d), x.dtype)
    h = jnp.concatenate([x, meta, zeros], axis=1).reshape(batch * lp, d)
    tables = _rope_tables(s)

    perm = _rope_perm()
    attn_w = _prep_weights(attn_w_in, perm_cols=ATTN_WIDTH + ATTN_KV_WIDTH)
    gla_w, gla_w_lr = _prep_gla_weights(jnp.swapaxes(gla_w_in, 1, 2))
    attn_wo = _prep_weights(attn_w_out)
    gla_wo = _prep_weights(gla_w_out)

    for i in range(depth):
        j = i // 2
        rows = s if i == depth - 1 else lp
        if i % 2 == 0:
            head_w = jnp.concatenate([
                jnp.tile(attn_q_norm[j][perm] * (HEAD_DIM ** -0.5 * LOG2_E), ATTN_HEADS),
                jnp.tile(attn_k_norm[j][perm], ATTN_KV_HEADS)]).reshape(1, -1)
            proj = _inproj(h, pre_norm[i], attn_w, j, ATTN_IN, lp=lp, rope=(head_w,) + tables)
            q_gain = jnp.max(jnp.abs(head_w[0, :ATTN_WIDTH]))
            k_gain = jnp.max(jnp.abs(head_w[0, ATTN_WIDTH:]))
            bounded = (HEAD_DIM * q_gain * k_gain <= SCORE_BOUND).astype(jnp.int32).reshape(1)
            o = _attention(proj, bounded, batch=batch, lp=lp)
            h = _outproj(o, attn_wo, j, h, post_norm[i], batch=batch, lp=lp, rows=rows,
                         gate_src=proj, gate_col_block=(ATTN_IN - ATTN_WIDTH) // ATTN_WIDTH)
        else:
            proj, lr = _inproj(h, pre_norm[i], gla_w, j, GLA_MAIN, lp=lp, w_lr=gla_w_lr,
                               w_transposed=True)
            o_f = _gla_direction(proj, lr, gla_gk_up[j, 0], gla_gk_bias[j, 0],
                                 batch=batch, lp=lp, backward=False)
            gated = _gla_direction(proj, lr, gla_gk_up[j, 1], gla_gk_bias[j, 1],
                                   batch=batch, lp=lp, backward=True,
                                   o_fwd=o_f, o_norm=gla_o_norm[j])
            h = _outproj(gated, gla_wo, j, h, post_norm[i], batch=batch, lp=lp, rows=rows)
    return h.reshape(batch, s, d)
```

```python
import functools

import jax
import jax.numpy as jnp
from jax import lax
from jax.experimental import pallas as pl
from jax.experimental.pallas import tpu as pltpu

F32 = jnp.float32
BF16 = jnp.bfloat16

D_MODEL = 2048
N_META = 16
GRID_W = 64
NORM_EPS = 1e-6
ROPE_THETA = 10000.0
LOG2_E = 1.4426950408889634

HEAD_DIM = 128
ATTN_HEADS = 16
ATTN_KV_HEADS = 8
ATTN_GROUP = ATTN_HEADS // ATTN_KV_HEADS
ATTN_WIDTH = ATTN_HEADS * HEAD_DIM
ATTN_KV_WIDTH = ATTN_KV_HEADS * HEAD_DIM
ATTN_IN = 2 * ATTN_WIDTH + 2 * ATTN_KV_WIDTH

GLA_HEADS = 4
GLA_KEY_DIM = D_MODEL // 2
GLA_VALUE_DIM = D_MODEL
GLA_HEAD_K = GLA_KEY_DIM // GLA_HEADS
GLA_HEAD_V = GLA_VALUE_DIM // GLA_HEADS
GLA_GATE_RANK = 16
GLA_GATE_NORMALIZER = 16.0
GLA_CHUNK = 64
GLA_MAIN = 2 * GLA_KEY_DIM + 2 * GLA_VALUE_DIM
GLA_IN = GLA_MAIN + 2 * GLA_GATE_RANK

LANES = 128
ROW_PAD = 128
GLA_BLOCK = 128
VMEM_LIMIT = 56 * 1024 * 1024

PROJ_TN = 1024
NORM_CHUNK = 128
WPREP_ROWS = 256
WPREP_ROWS_T = 512
KV_TILE = 1024
SCORE_BOUND = 60.0


def _pick_tile(n, candidates):
    for c in candidates:
        if n % c == 0:
            return c
    raise ValueError(f"no tile in {candidates} divides {n}")


def _split_hi_lo(a):
    hi = a.astype(BF16)
    lo = (a - hi.astype(F32)).astype(BF16)
    return hi, lo


def _inproj_kernel(n_i, n_j, n_rope_tiles, has_lr, tm, tn, w_transposed, k_chunks, *refs):
    it = iter(refs)
    h_refs = [next(it) for _ in range(k_chunks)]
    pw_ref, w_ref = next(it), next(it)
    wlr_ref = next(it) if has_lr else None
    if n_rope_tiles:
        hn_ref, c_ref, sn_ref = next(it), next(it), next(it)
    o_ref = next(it)
    lr_ref = next(it) if has_lr else None
    xn_sc, xn_next_sc, acc_sc = next(it), next(it), next(it)

    t = pl.program_id(0)
    u = t - n_j
    n_tiles = n_i * n_j

    def project(x, w):
        contract = (((1,), (1,)), ((), ())) if w_transposed else (((1,), (0,)), ((), ()))
        return lax.dot_general(x, w, contract, preferred_element_type=F32)

    def normalise_chunk():
        for c, h_ref in enumerate(h_refs):
            chunk = jnp.minimum((t % n_j) * k_chunks + c, tm // NORM_CHUNK - 1)
            start = pl.multiple_of(chunk * NORM_CHUNK, NORM_CHUNK)
            x = h_ref[...]
            ms = jnp.mean(x * x, axis=-1, keepdims=True)
            xn = (x * lax.rsqrt(ms + NORM_EPS) * pw_ref[...]).astype(BF16)
            xn_next_sc[pl.ds(start, NORM_CHUNK), :] = xn
            if has_lr:
                lr_ref[pl.ds(start, NORM_CHUNK), :] = project(xn, wlr_ref[...])

    def finish(acc, rope_tile):
        if not rope_tile:
            o_ref[...] = acc.astype(BF16)
            return
        c, sn = c_ref[...], sn_ref[...]
        for hh in range(tn // HEAD_DIM):
            cols = slice(hh * HEAD_DIM, (hh + 1) * HEAD_DIM)
            a = acc[:, cols]
            r = lax.rsqrt(jnp.mean(a * a, axis=-1, keepdims=True) + NORM_EPS)
            y = a * r * hn_ref[:, cols]
            y = y * c + pltpu.roll(y, HEAD_DIM // 2, 1) * sn
            o_ref[:, cols] = y.astype(BF16)

    def body(rope_tile):
        prev = acc_sc[...]
        acc_sc[...] = project(xn_sc[...], w_ref[...])
        finish(prev, rope_tile)
        normalise_chunk()

    @pl.when(t == 0)
    def _():
        acc_sc[...] = jnp.zeros_like(acc_sc)

    @pl.when(u < 0)
    def _():
        normalise_chunk()

    in_body = (u >= 0) & (u < n_tiles)

    @pl.when(in_body & (u % n_j == 0))
    def _():
        xn_sc[...] = xn_next_sc[...]

    prev_is_rope = (jnp.maximum(u - 1, 0) % n_j) < n_rope_tiles
    if n_rope_tiles:
        @pl.when(in_body & prev_is_rope)
        def _():
            body(True)

    @pl.when(in_body & jnp.logical_not(prev_is_rope))
    def _():
        body(False)

    @pl.when(u == n_tiles)
    def _():
        finish(acc_sc[...], False)


def _inproj(h, pre_w, w, layer, n, *, lp, w_lr=None, rope=None, w_transposed=False):
    np_, d = h.shape
    tm = _pick_tile(lp, (1408, 384, 128))
    tn = PROJ_TN
    n_i, n_j = np_ // tm, n // tn
    n_tiles = n_i * n_j
    chunks_per_tile = tm // NORM_CHUNK
    k_chunks = pl.cdiv(chunks_per_tile, n_j)
    tiles_per_batch = lp // tm
    has_lr = w_lr is not None
    n_rope_tiles = 0

    def tile(t):
        u = jnp.clip(t - n_j, 0, n_tiles - 1)
        return u // n_j, u % n_j

    def prev_tile(t):
        u = jnp.clip(t - n_j - 1, 0, n_tiles - 1)
        return u // n_j, u % n_j

    def norm_row(t):
        return jnp.minimum(t // n_j, n_i - 1)

    def h_chunk(c):
        def index(t):
            chunk = jnp.minimum((t % n_j) * k_chunks + c, chunks_per_tile - 1)
            return norm_row(t) * chunks_per_tile + chunk, 0
        return pl.BlockSpec((NORM_CHUNK, d), index)

    in_specs = [h_chunk(c) for c in range(k_chunks)] + [
        pl.BlockSpec((1, d), lambda t: (0, 0)),
        (pl.BlockSpec((None, tn, d), lambda t: (layer, tile(t)[1], 0)) if w_transposed
         else pl.BlockSpec((None, d, tn), lambda t: (layer, 0, tile(t)[1]))),
    ]
    args = [h] * k_chunks + [pre_w.reshape(1, d), w]
    if has_lr:
        in_specs.append(pl.BlockSpec((None, LANES, d), lambda t: (layer, 0, 0)) if w_transposed
                        else pl.BlockSpec((None, d, LANES), lambda t: (layer, 0, 0)))
        args.append(w_lr)
    if rope is not None:
        head_w, cos_t, sin_t = rope
        n_rope_tiles = head_w.shape[1] // tn
        last = n_rope_tiles - 1
        in_specs.append(pl.BlockSpec((1, tn), lambda t: (0, jnp.minimum(prev_tile(t)[1], last))))
        tab = pl.BlockSpec((tm, HEAD_DIM), lambda t: (prev_tile(t)[0] % tiles_per_batch, 0))
        in_specs += [tab, tab]
        args += [head_w, cos_t, sin_t]
    out_shape = [jax.ShapeDtypeStruct((np_, n), BF16)]
    out_specs = [pl.BlockSpec((tm, tn), lambda t: prev_tile(t))]
    if has_lr:
        out_shape.append(jax.ShapeDtypeStruct((np_, LANES), F32))
        out_specs.append(pl.BlockSpec((tm, LANES), lambda t: (norm_row(t), 0)))
    res = pl.pallas_call(
        functools.partial(_inproj_kernel, n_i, n_j, n_rope_tiles, has_lr, tm, tn, w_transposed,
                          k_chunks),
        grid=(n_tiles + n_j + 1,),
        in_specs=in_specs,
        out_specs=out_specs,
        out_shape=out_shape,
        scratch_shapes=[pltpu.VMEM((tm, d), BF16), pltpu.VMEM((tm, d), BF16),
                        pltpu.VMEM((tm, tn), F32)],
        compiler_params=pltpu.CompilerParams(
            dimension_semantics=("arbitrary",), vmem_limit_bytes=VMEM_LIMIT),
        name="inproj_rope" if rope is not None else "inproj_gla",
    )(*args)
    return res if has_lr else res[0]


def _attn_kernel(s_real, tq, nq, n_tiles, bounded_ref, q_ref, k_ref, v_ref, o_ref,
                 vt_sc, m_sc, l_sc, acc_sc, s_a, s_b, cm_a, cm_b):
    lp = s_real + ROW_PAD
    s_bufs, cm_bufs = (s_a, s_b), (cm_a, cm_b)
    t = pl.program_id(0)

    @pl.when(t == 0)
    def _():
        acc_sc[...] = jnp.zeros_like(acc_sc)
        l_sc[...] = jnp.ones_like(l_sc)

    @pl.when((t % nq == 0) & (t < n_tiles))
    def _():
        for c in range(lp // LANES):
            cols = slice(c * LANES, (c + 1) * LANES)
            vt_sc[:, cols] = v_ref[cols, :].astype(F32).T.astype(BF16)

    def finish(acc, l):
        o = (acc / l).T
        for g in range(ATTN_GROUP):
            o_ref[:, g * HEAD_DIM:(g + 1) * HEAD_DIM] = o[g * tq:(g + 1) * tq].astype(BF16)

    n_full = s_real // KV_TILE
    chunks = [(c * KV_TILE, KV_TILE, None) for c in range(n_full - 1)]
    chunks.append(((n_full - 1) * KV_TILE, KV_TILE + ROW_PAD, KV_TILE + N_META))

    def stacked_queries():
        return jnp.concatenate(
            [q_ref[:, g * HEAD_DIM:(g + 1) * HEAD_DIM] for g in range(ATTN_GROUP)], axis=0)

    def key_scores(q, idx):
        r0, rows, _ = chunks[idx]
        return lax.dot_general(k_ref[r0:r0 + rows, :], q, (((1,), (1,)), ((), ())),
                               preferred_element_type=F32)

    def mask_tail(idx, x, fill):
        _, rows, n_valid = chunks[idx]
        if n_valid is None:
            return x
        row = lax.broadcasted_iota(jnp.int32, (ROW_PAD, 1), 0) + (rows - ROW_PAD)
        tail = jnp.where(row < n_valid, x[rows - ROW_PAD:], fill)
        return jnp.concatenate([x[:rows - ROW_PAD], tail], axis=0)

    @pl.when((bounded_ref[0] != 0) & (t < n_tiles))
    def _():
        prev_acc, prev_l = acc_sc[...], l_sc[...]
        q = stacked_queries()
        acc = jnp.zeros(acc_sc.shape, F32)
        l = jnp.zeros(l_sc.shape, F32)
        for idx, (r0, rows, _) in enumerate(chunks):
            p = mask_tail(idx, jnp.exp2(key_scores(q, idx)), 0.0)
            l = l + jnp.sum(p, axis=0, keepdims=True)
            acc = acc + jnp.dot(vt_sc[:, r0:r0 + rows], p.astype(BF16),
                                preferred_element_type=F32)
        acc_sc[...] = acc
        l_sc[...] = l
        finish(prev_acc, prev_l)

    @pl.when((bounded_ref[0] == 0) & (t < n_tiles))
    def _():
        finish(acc_sc[...], l_sc[...])
        q = stacked_queries()
        m_sc[...] = jnp.full_like(m_sc, -jnp.inf)
        l_sc[...] = jnp.zeros_like(l_sc)
        acc_sc[...] = jnp.zeros_like(acc_sc)

        def scores(idx):
            rows = chunks[idx][1]
            s = mask_tail(idx, key_scores(q, idx), -jnp.inf)
            s_bufs[idx % 2][0:rows, :] = s
            cm_bufs[idx % 2][...] = jnp.max(s, axis=0, keepdims=True)

        def accumulate(idx):
            r0, rows, _ = chunks[idx]
            m_prev = m_sc[...]
            m_new = jnp.maximum(m_prev, cm_bufs[idx % 2][...])
            alpha = jnp.exp2(m_prev - m_new)
            p = jnp.exp2(s_bufs[idx % 2][0:rows, :] - m_new)
            l_sc[...] = alpha * l_sc[...] + jnp.sum(p, axis=0, keepdims=True)
            acc_sc[...] = alpha * acc_sc[...] + jnp.dot(
                vt_sc[:, r0:r0 + rows], p.astype(BF16), preferred_element_type=F32)
            m_sc[...] = m_new

        scores(0)
        for c in range(len(chunks)):
            if c + 1 < len(chunks):
                scores(c + 1)
            accumulate(c)

    @pl.when(t == n_tiles)
    def _():
        finish(acc_sc[...], l_sc[...])


def _attention(proj, bounded, *, batch, lp):
    np_ = proj.shape[0]
    s_real = lp - ROW_PAD
    tq = _pick_tile(lp, (384, 128))
    nq = lp // tq
    n_tiles = batch * ATTN_KV_HEADS * nq
    gw = ATTN_GROUP * HEAD_DIM
    k_col0 = ATTN_WIDTH // HEAD_DIM
    v_col0 = (ATTN_WIDTH + ATTN_KV_WIDTH) // HEAD_DIM

    def tile(t):
        t = jnp.minimum(t, n_tiles - 1)
        return t // (ATTN_KV_HEADS * nq), (t // nq) % ATTN_KV_HEADS, t % nq

    def q_block(t):
        b, h, i = tile(t)
        return b * nq + i, h

    return pl.pallas_call(
        functools.partial(_attn_kernel, s_real, tq, nq, n_tiles),
        grid=(n_tiles + 1,),
        in_specs=[
            pl.BlockSpec(memory_space=pltpu.SMEM),
            pl.BlockSpec((tq, gw), q_block),
            pl.BlockSpec((lp, HEAD_DIM), lambda t: (tile(t)[0], k_col0 + tile(t)[1])),
            pl.BlockSpec((lp, HEAD_DIM), lambda t: (tile(t)[0], v_col0 + tile(t)[1])),
        ],
        out_specs=pl.BlockSpec((tq, gw), lambda t: q_block(jnp.maximum(t - 1, 0))),
        out_shape=jax.ShapeDtypeStruct((np_, ATTN_WIDTH), BF16),
        scratch_shapes=[
            pltpu.VMEM((HEAD_DIM, lp), BF16),
            pltpu.VMEM((1, ATTN_GROUP * tq), F32),
            pltpu.VMEM((1, ATTN_GROUP * tq), F32),
            pltpu.VMEM((HEAD_DIM, ATTN_GROUP * tq), F32),
            pltpu.VMEM((KV_TILE + ROW_PAD, ATTN_GROUP * tq), F32),
            pltpu.VMEM((KV_TILE + ROW_PAD, ATTN_GROUP * tq), F32),
            pltpu.VMEM((1, ATTN_GROUP * tq), F32),
            pltpu.VMEM((1, ATTN_GROUP * tq), F32),
        ],
        compiler_params=pltpu.CompilerParams(
            dimension_semantics=("arbitrary",), vmem_limit_bytes=VMEM_LIMIT),
        name="attention",
    )(bounded, proj, proj, proj)


def _outproj_kernel(has_gate, *refs):
    it = iter(refs)
    x_ref = next(it)
    g_ref = next(it) if has_gate else None
    w_ref, h_ref, pw_ref, out_ref = next(it), next(it), next(it), next(it)
    x = x_ref[...]
    if has_gate:
        g = g_ref[...].astype(F32)
        x = (x.astype(F32) * (g * jax.nn.sigmoid(g))).astype(BF16)
    y = jnp.dot(x, w_ref[...], preferred_element_type=F32)
    r = lax.rsqrt(jnp.mean(y * y, axis=-1, keepdims=True) + NORM_EPS)
    out_ref[...] = h_ref[...] + y * r * pw_ref[...]


def _outproj(x, w, layer, h, post_w, *, batch, lp, rows, gate_src=None, gate_col_block=None):
    kdim = x.shape[1]
    d = w.shape[2]
    tm = _pick_tile(rows, (512, 384, 128))
    has_gate = gate_src is not None
    in_specs = [pl.BlockSpec((None, tm, kdim), lambda b, i: (b, i, 0))]
    args = [x.reshape(batch, lp, kdim)]
    if has_gate:
        in_specs.append(pl.BlockSpec((None, tm, kdim), lambda b, i: (b, i, gate_col_block)))
        args.append(gate_src.reshape(batch, lp, gate_src.shape[1]))
    in_specs += [
        pl.BlockSpec((None, kdim, d), lambda b, i: (layer, 0, 0)),
        pl.BlockSpec((None, tm, d), lambda b, i: (b, i, 0)),
        pl.BlockSpec((1, d), lambda b, i: (0, 0)),
    ]
    args += [w, h.reshape(batch, lp, d), post_w.reshape(1, d)]
    out = pl.pallas_call(
        functools.partial(_outproj_kernel, has_gate),
        grid=(batch, rows // tm),
        in_specs=in_specs,
        out_specs=pl.BlockSpec((None, tm, d), lambda b, i: (b, i, 0)),
        out_shape=jax.ShapeDtypeStruct((batch, rows, d), F32),
        compiler_params=pltpu.CompilerParams(
            dimension_semantics=("parallel", "parallel"), vmem_limit_bytes=VMEM_LIMIT),
        name="outproj_gated" if has_gate else "outproj",
    )(*args)
    return out.reshape(batch * rows, d)


def _gla_kernel(backward, n_blocks, *refs):
    it = iter(refs)
    q_ref, k_ref, v_ref, lr_ref, up_ref, bias_ref = (next(it) for _ in range(6))
    if backward:
        of_ref, gate_ref, onorm_ref = next(it), next(it), next(it)
    out_ref, st_sc = next(it), next(it)

    i = pl.program_id(1)

    @pl.when(i == 0)
    def _():
        st_sc[...] = jnp.zeros_like(st_sc)

    is_meta = (i == n_blocks - 1) if backward else (i == 0)
    n_valid = jnp.where(is_meta, N_META, GLA_BLOCK)

    C, R = GLA_CHUNK, GLA_BLOCK
    row = lax.broadcasted_iota(jnp.int32, (R, R), 0)
    col = lax.broadcasted_iota(jnp.int32, (R, R), 1)
    same_chunk = (row // C) == (col // C)
    if backward:
        cum_mat = (same_chunk & (col >= row)).astype(BF16)
        own_mask = same_chunk & (col > row)
        cross_mask = (row < C) & (col >= C)
        e0, e1 = 0, C
    else:
        cum_mat = (same_chunk & (col <= row)).astype(BF16)
        own_mask = same_chunk & (col <= row)
        cross_mask = (row >= C) & (col < C)
        e0, e1 = C - 1, R - 1
    gate_cols = slice(GLA_GATE_RANK, 2 * GLA_GATE_RANK) if backward else slice(0, GLA_GATE_RANK)

    rows1 = lax.broadcasted_iota(jnp.int32, (R, 1), 0)
    valid = rows1 < n_valid
    first = rows1 < C

    z = jnp.dot(lr_ref[:, gate_cols].astype(BF16), up_ref[...].astype(BF16),
                preferred_element_type=F32) + bias_ref[...]
    log_sig = jnp.minimum(z, 0.0) - jnp.log1p(jnp.exp(-jnp.abs(z)))
    g = jnp.where(valid, log_sig / GLA_GATE_NORMALIZER, 0.0)
    g_hi, g_lo = _split_hi_lo(g)
    b = (jnp.dot(cum_mat, g_hi, preferred_element_type=F32)
         + jnp.dot(cum_mat, g_lo, preferred_element_type=F32))
    tot0, tot1 = b[e0:e0 + 1, :], b[e1:e1 + 1, :]
    tot_own = jnp.where(first, tot0, tot1)
    q = q_ref[...].astype(F32)
    k = jnp.where(valid, k_ref[...].astype(F32), 0.0)
    v_all = jnp.where(valid, v_ref[...], jnp.zeros((), BF16))
    qd = q * jnp.exp(b)
    ki = (k * jnp.exp(-b)).astype(BF16)
    ke = k * jnp.exp(tot_own - b)
    if backward:
        q_st = jnp.concatenate([qd[:C] * jnp.exp(tot1), qd[C:]], axis=0)
        ke_st = jnp.concatenate([ke[:C], ke[C:] * jnp.exp(tot0)], axis=0)
    else:
        q_st = jnp.concatenate([qd[:C], qd[C:] * jnp.exp(tot0)], axis=0)
        ke_st = jnp.concatenate([ke[:C] * jnp.exp(tot1), ke[C:]], axis=0)
    qd, ke = qd.astype(BF16), ke.astype(BF16)
    q_st, ke_st = q_st.astype(BF16), ke_st.astype(BF16)
    decay = jnp.exp(tot0 + tot1)

    for hd in range(GLA_HEADS):
        kc = slice(hd * GLA_HEAD_K, (hd + 1) * GLA_HEAD_K)
        vc = slice(hd * GLA_HEAD_V, (hd + 1) * GLA_HEAD_V)
        v = v_all[:, vc]
        keys = jnp.concatenate([ki[:, kc], ke[:, kc]], axis=0)
        p = lax.dot_general(qd[:, kc], keys, (((1,), (1,)), ((), ())),
                            preferred_element_type=F32)
        sc = (jnp.where(own_mask, p[:, :R], 0.0)
              + jnp.where(cross_mask, p[:, R:], 0.0)).astype(BF16)
        st = st_sc[hd]
        o = (jnp.dot(sc, v, preferred_element_type=F32)
             + lax.dot_general(q_st[:, kc], st.astype(BF16), (((1,), (1,)), ((), ())),
                               preferred_element_type=F32))
        st_sc[hd] = st * decay[:, kc] + lax.dot_general(
            v, ke_st[:, kc], (((0,), (0,)), ((), ())), preferred_element_type=F32)
        if backward:
            tot = o + of_ref[:, vc].astype(F32)
            r = lax.rsqrt(jnp.mean(tot * tot, axis=-1, keepdims=True) + NORM_EPS)
            gt = gate_ref[:, vc].astype(F32)
            out_ref[:, vc] = (tot * r * onorm_ref[...] * (gt * jax.nn.sigmoid(gt))).astype(BF16)
        else:
            out_ref[:, vc] = o.astype(BF16)


def _gla_direction(proj, lr, up, bias, *, batch, lp, backward, o_fwd=None, o_norm=None):
    np_ = proj.shape[0]
    nb = lp // GLA_BLOCK
    kb = GLA_KEY_DIM

    if backward:
        def blk(b, i):
            return b * nb + jnp.where(i == nb - 1, nb - 1, nb - 2 - i)
    else:
        def blk(b, i):
            return b * nb + jnp.where(i == 0, nb - 1, i - 1)

    in_specs = [
        pl.BlockSpec((GLA_BLOCK, kb), lambda b, i: (blk(b, i), 0)),
        pl.BlockSpec((GLA_BLOCK, kb), lambda b, i: (blk(b, i), 1)),
        pl.BlockSpec((GLA_BLOCK, GLA_VALUE_DIM), lambda b, i: (blk(b, i), 1)),
        pl.BlockSpec((GLA_BLOCK, LANES), lambda b, i: (blk(b, i), 0)),
        pl.BlockSpec((GLA_GATE_RANK, kb), lambda b, i: (0, 0)),
        pl.BlockSpec((1, kb), lambda b, i: (0, 0)),
    ]
    args = [proj, proj, proj, lr, up, bias.reshape(1, kb)]
    if backward:
        in_specs += [
            pl.BlockSpec((GLA_BLOCK, GLA_VALUE_DIM), lambda b, i: (blk(b, i), 0)),
            pl.BlockSpec((GLA_BLOCK, GLA_VALUE_DIM), lambda b, i: (blk(b, i), 2)),
            pl.BlockSpec((1, GLA_HEAD_V), lambda b, i: (0, 0)),
        ]
        args += [o_fwd, proj, o_norm.reshape(1, GLA_HEAD_V)]
    return pl.pallas_call(
        functools.partial(_gla_kernel, backward, nb),
        grid=(batch, nb),
        in_specs=in_specs,
        out_specs=pl.BlockSpec((GLA_BLOCK, GLA_VALUE_DIM), lambda b, i: (blk(b, i), 0)),
        out_shape=jax.ShapeDtypeStruct((np_, GLA_VALUE_DIM), BF16),
        scratch_shapes=[pltpu.VMEM((GLA_HEADS, GLA_HEAD_V, GLA_HEAD_K), F32)],
        compiler_params=pltpu.CompilerParams(
            dimension_semantics=("parallel", "arbitrary"), vmem_limit_bytes=VMEM_LIMIT),
        name="gla_bwd" if backward else "gla_fwd",
    )(*args)


def _wprep_kernel(perm_cols, x_ref, o_ref):
    if perm_cols:
        group = (lax.broadcasted_iota(jnp.int32, (1, HEAD_DIM), 1) * 4) // HEAD_DIM
        for hh in range(perm_cols // HEAD_DIM):
            cols = slice(hh * HEAD_DIM, (hh + 1) * HEAD_DIM)
            xh = x_ref[:, cols]
            up32 = pltpu.roll(xh, HEAD_DIM - HEAD_DIM // 4, 1)
            down32 = pltpu.roll(xh, HEAD_DIM // 4, 1)
            o_ref[:, cols] = jnp.where(group == 1, up32,
                                       jnp.where(group == 2, down32, xh)).astype(BF16)
    o_ref[:, perm_cols:] = x_ref[:, perm_cols:].astype(BF16)


def _prep_weights(w, *, perm_cols=0):
    layers, k, n = w.shape
    tr = WPREP_ROWS
    return pl.pallas_call(
        functools.partial(_wprep_kernel, perm_cols),
        grid=(layers, k // tr),
        in_specs=[pl.BlockSpec((None, tr, n), lambda l, r: (l, r, 0))],
        out_specs=pl.BlockSpec((None, tr, n), lambda l, r: (l, r, 0)),
        out_shape=jax.ShapeDtypeStruct((layers, k, n), BF16),
        compiler_params=pltpu.CompilerParams(
            dimension_semantics=("parallel", "parallel"), vmem_limit_bytes=VMEM_LIMIT),
        name="weight_prep",
    )(w)


def _wprep_gla_kernel(scaled_tiles, scale, x_ref, tail_ref, o_ref, otail_ref):
    r = pl.program_id(1)
    o_ref[...] = (x_ref[...] * jnp.where(r < scaled_tiles, scale, 1.0)).astype(BF16)
    tail = tail_ref[...].astype(BF16)
    pad = jnp.zeros((LANES - tail.shape[0], tail.shape[1]), BF16)
    otail_ref[...] = jnp.concatenate([tail, pad], axis=0)


def _prep_gla_weights(w_t):
    layers, n_in, d = w_t.shape
    tail = n_in - GLA_MAIN
    tr = WPREP_ROWS_T
    return pl.pallas_call(
        functools.partial(_wprep_gla_kernel, GLA_KEY_DIM // tr, GLA_HEAD_K ** -0.5),
        grid=(layers, GLA_MAIN // tr),
        in_specs=[pl.BlockSpec((None, tr, d), lambda l, r: (l, r, 0)),
                  pl.BlockSpec((None, tail, d), lambda l, r: (l, GLA_MAIN // tail, 0))],
        out_specs=[pl.BlockSpec((None, tr, d), lambda l, r: (l, r, 0)),
                   pl.BlockSpec((None, LANES, d), lambda l, r: (l, 0, 0))],
        out_shape=[jax.ShapeDtypeStruct((layers, GLA_MAIN, d), BF16),
                   jax.ShapeDtypeStruct((layers, LANES, d), BF16)],
        compiler_params=pltpu.CompilerParams(
            dimension_semantics=("parallel", "arbitrary"), vmem_limit_bytes=VMEM_LIMIT),
        name="weight_prep_gla",
    )(w_t, w_t)


def _rope_perm():
    quarter = HEAD_DIM // 4
    idx = jnp.arange(HEAD_DIM).reshape(2, 2, quarter)
    return idx.transpose(1, 0, 2).reshape(HEAD_DIM)


def _rope_tables(s):
    rows = s // GRID_W
    row = jnp.repeat(jnp.arange(rows), GRID_W).astype(F32)
    col = jnp.tile(jnp.arange(GRID_W), rows).astype(F32)
    quarter = HEAD_DIM // 4
    inv_freq = ROPE_THETA ** (-jnp.arange(0, 2 * quarter, 2, dtype=F32) / (2 * quarter))
    ang_row = row[:, None] * inv_freq[None]
    ang_col = col[:, None] * inv_freq[None]
    ang = jnp.concatenate([ang_row, ang_col, ang_row, ang_col], axis=-1)
    ang = jnp.concatenate([ang, jnp.zeros((ROW_PAD, HEAD_DIM), F32)], axis=0)
    cos, sin = jnp.cos(ang), jnp.sin(ang)
    first_half = jnp.arange(HEAD_DIM) < HEAD_DIM // 2
    return cos, jnp.where(first_half[None], -sin, sin)


def kernel(x, meta_tokens, pre_norm, post_norm, attn_w_in, attn_q_norm, attn_k_norm,
           attn_w_out, gla_w_in, gla_gk_up, gla_gk_bias, gla_o_norm, gla_w_out):
    batch, s, d = x.shape
    assert d == D_MODEL and s % GLA_BLOCK == 0 and s % KV_TILE == 0 and s % GRID_W == 0
    lp = s + ROW_PAD
    depth = pre_norm.shape[0]

    meta = jnp.broadcast_to(meta_tokens.astype(x.dtype)[None], (batch, N_META, d))
    zeros = jnp.zeros((batch, ROW_PAD - N_META, d), x.dtype)
    h = jnp.concatenate([x, meta, zeros], axis=1).reshape(batch * lp, d)
    tables = _rope_tables(s)

    perm = _rope_perm()
    attn_w = _prep_weights(attn_w_in, perm_cols=ATTN_WIDTH + ATTN_KV_WIDTH)
    gla_w, gla_w_lr = _prep_gla_weights(jnp.swapaxes(gla_w_in, 1, 2))
    attn_wo = _prep_weights(attn_w_out)
    gla_wo = _prep_weights(gla_w_out)

    for i in range(depth):
        j = i // 2
        rows = s if i == depth - 1 else lp
        if i % 2 == 0:
            head_w = jnp.concatenate([
                jnp.tile(attn_q_norm[j][perm] * (HEAD_DIM ** -0.5 * LOG2_E), ATTN_HEADS),
                jnp.tile(attn_k_norm[j][perm], ATTN_KV_HEADS)]).reshape(1, -1)
            proj = _inproj(h, pre_norm[i], attn_w, j, ATTN_IN, lp=lp, rope=(head_w,) + tables)
            q_gain = jnp.max(jnp.abs(head_w[0, :ATTN_WIDTH]))
            k_gain = jnp.max(jnp.abs(head_w[0, ATTN_WIDTH:]))
            bounded = (HEAD_DIM * q_gain * k_gain <= SCORE_BOUND).astype(jnp.int32).reshape(1)
            o = _attention(proj, bounded, batch=batch, lp=lp)
            h = _outproj(o, attn_wo, j, h, post_norm[i], batch=batch, lp=lp, rows=rows,
                         gate_src=proj, gate_col_block=(ATTN_IN - ATTN_WIDTH) // ATTN_WIDTH)
        else:
            proj, lr = _inproj(h, pre_norm[i], gla_w, j, GLA_MAIN, lp=lp, w_lr=gla_w_lr,
                               w_transposed=True)
            o_f = _gla_direction(proj, lr, gla_gk_up[j, 0], gla_gk_bias[j, 0],
                                 batch=batch, lp=lp, backward=False)
            gated = _gla_direction(proj, lr, gla_gk_up[j, 1], gla_gk_bias[j, 1],
                                   batch=batch, lp=lp, backward=True,
                                   o_fwd=o_f, o_norm=gla_o_norm[j])
            h = _outproj(gated, gla_wo, j, h, post_norm[i], batch=batch, lp=lp, rows=rows)
    return h.reshape(batch, s, d)
```

```python
import functools

import jax
import jax.numpy as jnp
from jax import lax
from jax.experimental import pallas as pl
from jax.experimental.pallas import tpu as pltpu

F32 = jnp.float32
BF16 = jnp.bfloat16

D_MODEL = 2048
N_META = 16
GRID_W = 64
NORM_EPS = 1e-6
ROPE_THETA = 10000.0
LOG2_E = 1.4426950408889634

HEAD_DIM = 128
ATTN_HEADS = 16
ATTN_KV_HEADS = 8
ATTN_GROUP = ATTN_HEADS // ATTN_KV_HEADS
ATTN_WIDTH = ATTN_HEADS * HEAD_DIM
ATTN_KV_WIDTH = ATTN_KV_HEADS * HEAD_DIM
ATTN_IN = 2 * ATTN_WIDTH + 2 * ATTN_KV_WIDTH

GLA_HEADS = 4
GLA_KEY_DIM = D_MODEL // 2
GLA_VALUE_DIM = D_MODEL
GLA_HEAD_K = GLA_KEY_DIM // GLA_HEADS
GLA_HEAD_V = GLA_VALUE_DIM // GLA_HEADS
GLA_GATE_RANK = 16
GLA_GATE_NORMALIZER = 16.0
GLA_CHUNK = 64
GLA_MAIN = 2 * GLA_KEY_DIM + 2 * GLA_VALUE_DIM
GLA_IN = GLA_MAIN + 2 * GLA_GATE_RANK

LANES = 128
ROW_PAD = 128
GLA_BLOCK = 128
VMEM_LIMIT = 56 * 1024 * 1024

PROJ_TN = 1024
NORM_CHUNK = 128
WPREP_ROWS = 256
WPREP_ROWS_T = 512
KV_TILE = 1024
SCORE_BOUND = 60.0


def _pick_tile(n, candidates):
    for c in candidates:
        if n % c == 0:
            return c
    raise ValueError(f"no tile in {candidates} divides {n}")


def _split_hi_lo(a):
    hi = a.astype(BF16)
    lo = (a - hi.astype(F32)).astype(BF16)
    return hi, lo


def _inproj_kernel(n_i, n_j, n_rope_tiles, has_lr, tm, tn, w_transposed, k_chunks, *refs):
    it = iter(refs)
    h_refs = [next(it) for _ in range(k_chunks)]
    pw_ref, w_ref = next(it), next(it)
    wlr_ref = next(it) if has_lr else None
    if n_rope_tiles:
        hn_ref, c_ref, sn_ref = next(it), next(it), next(it)
    o_ref = next(it)
    lr_ref = next(it) if has_lr else None
    xn_sc, xn_next_sc, acc_sc = next(it), next(it), next(it)

    t = pl.program_id(0)
    u = t - n_j
    n_tiles = n_i * n_j

    def project(x, w):
        contract = (((1,), (1,)), ((), ())) if w_transposed else (((1,), (0,)), ((), ()))
        return lax.dot_general(x, w, contract, preferred_element_type=F32)

    def normalise_chunk():
        for c, h_ref in enumerate(h_refs):
            chunk = jnp.minimum((t % n_j) * k_chunks + c, tm // NORM_CHUNK - 1)
            start = pl.multiple_of(chunk * NORM_CHUNK, NORM_CHUNK)
            x = h_ref[...]
            ms = jnp.mean(x * x, axis=-1, keepdims=True)
            xn = (x * lax.rsqrt(ms + NORM_EPS) * pw_ref[...]).astype(BF16)
            xn_next_sc[pl.ds(start, NORM_CHUNK), :] = xn
            if has_lr:
                lr_ref[pl.ds(start, NORM_CHUNK), :] = project(xn, wlr_ref[...])

    def finish(acc, rope_tile):
        if not rope_tile:
            o_ref[...] = acc.astype(BF16)
            return
        c, sn = c_ref[...], sn_ref[...]
        for hh in range(tn // HEAD_DIM):
            cols = slice(hh * HEAD_DIM, (hh + 1) * HEAD_DIM)
            a = acc[:, cols]
            r = lax.rsqrt(jnp.mean(a * a, axis=-1, keepdims=True) + NORM_EPS)
            y = a * r * hn_ref[:, cols]
            y = y * c + pltpu.roll(y, HEAD_DIM // 2, 1) * sn
            o_ref[:, cols] = y.astype(BF16)

    def body(rope_tile):
        prev = acc_sc[...]
        acc_sc[...] = project(xn_sc[...], w_ref[...])
        finish(prev, rope_tile)
        normalise_chunk()

    @pl.when(t == 0)
    def _():
        acc_sc[...] = jnp.zeros_like(acc_sc)

    @pl.when(u < 0)
    def _():
        normalise_chunk()

    in_body = (u >= 0) & (u < n_tiles)

    @pl.when(in_body & (u % n_j == 0))
    def _():
        xn_sc[...] = xn_next_sc[...]

    prev_is_rope = (jnp.maximum(u - 1, 0) % n_j) < n_rope_tiles
    if n_rope_tiles:
        @pl.when(in_body & prev_is_rope)
        def _():
            body(True)

    @pl.when(in_body & jnp.logical_not(prev_is_rope))
    def _():
        body(False)

    @pl.when(u == n_tiles)
    def _():
        finish(acc_sc[...], False)


def _inproj(h, pre_w, w, layer, n, *, lp, w_lr=None, rope=None, w_transposed=False):
    np_, d = h.shape
    tm = _pick_tile(lp, (1408, 384, 128))
    tn = PROJ_TN
    n_i, n_j = np_ // tm, n // tn
    n_tiles = n_i * n_j
    chunks_per_tile = tm // NORM_CHUNK
    k_chunks = pl.cdiv(chunks_per_tile, n_j)
    tiles_per_batch = lp // tm
    has_lr = w_lr is not None
    n_rope_tiles = 0

    def tile(t):
        u = jnp.clip(t - n_j, 0, n_tiles - 1)
        return u // n_j, u % n_j

    def prev_tile(t):
        u = jnp.clip(t - n_j - 1, 0, n_tiles - 1)
        return u // n_j, u % n_j

    def norm_row(t):
        return jnp.minimum(t // n_j, n_i - 1)

    def h_chunk(c):
        def index(t):
            chunk = jnp.minimum((t % n_j) * k_chunks + c, chunks_per_tile - 1)
            return norm_row(t) * chunks_per_tile + chunk, 0
        return pl.BlockSpec((NORM_CHUNK, d), index)

    in_specs = [h_chunk(c) for c in range(k_chunks)] + [
        pl.BlockSpec((1, d), lambda t: (0, 0)),
        (pl.BlockSpec((None, tn, d), lambda t: (layer, tile(t)[1], 0)) if w_transposed
         else pl.BlockSpec((None, d, tn), lambda t: (layer, 0, tile(t)[1]))),
    ]
    args = [h] * k_chunks + [pre_w.reshape(1, d), w]
    if has_lr:
        in_specs.append(pl.BlockSpec((None, LANES, d), lambda t: (layer, 0, 0)) if w_transposed
                        else pl.BlockSpec((None, d, LANES), lambda t: (layer, 0, 0)))
        args.append(w_lr)
    if rope is not None:
        head_w, cos_t, sin_t = rope
        n_rope_tiles = head_w.shape[1] // tn
        last = n_rope_tiles - 1
        in_specs.append(pl.BlockSpec((1, tn), lambda t: (0, jnp.minimum(prev_tile(t)[1], last))))
        tab = pl.BlockSpec((tm, HEAD_DIM), lambda t: (prev_tile(t)[0] % tiles_per_batch, 0))
        in_specs += [tab, tab]
        args += [head_w, cos_t, sin_t]
    out_shape = [jax.ShapeDtypeStruct((np_, n), BF16)]
    out_specs = [pl.BlockSpec((tm, tn), lambda t: prev_tile(t))]
    if has_lr:
        out_shape.append(jax.ShapeDtypeStruct((np_, LANES), F32))
        out_specs.append(pl.BlockSpec((tm, LANES), lambda t: (norm_row(t), 0)))
    res = pl.pallas_call(
        functools.partial(_inproj_kernel, n_i, n_j, n_rope_tiles, has_lr, tm, tn, w_transposed,
                          k_chunks),
        grid=(n_tiles + n_j + 1,),
        in_specs=in_specs,
        out_specs=out_specs,
        out_shape=out_shape,
        scratch_shapes=[pltpu.VMEM((tm, d), BF16), pltpu.VMEM((tm, d), BF16),
                        pltpu.VMEM((tm, tn), F32)],
        compiler_params=pltpu.CompilerParams(
            dimension_semantics=("arbitrary",), vmem_limit_bytes=VMEM_LIMIT),
        name="inproj_rope" if rope is not None else "inproj_gla",
    )(*args)
    return res if has_lr else res[0]


def _attn_kernel(s_real, tq, nq, n_tiles, bounded_ref, q_ref, k_ref, v_ref, o_ref,
                 vt_sc, m_sc, l_sc, acc_sc, s_a, s_b, cm_a, cm_b):
    lp = s_real + ROW_PAD
    s_bufs, cm_bufs = (s_a, s_b), (cm_a, cm_b)
    t = pl.program_id(0)

    @pl.when(t == 0)
    def _():
        acc_sc[...] = jnp.zeros_like(acc_sc)
        l_sc[...] = jnp.ones_like(l_sc)

    @pl.when((t % nq == 0) & (t < n_tiles))
    def _():
        for c in range(lp // LANES):
            cols = slice(c * LANES, (c + 1) * LANES)
            vt_sc[:, cols] = v_ref[cols, :].astype(F32).T.astype(BF16)

    def finish(acc, l):
        o = (acc / l).T
        for g in range(ATTN_GROUP):
            o_ref[:, g * HEAD_DIM:(g + 1) * HEAD_DIM] = o[g * tq:(g + 1) * tq].astype(BF16)

    n_full = s_real // KV_TILE
    chunks = [(c * KV_TILE, KV_TILE, None) for c in range(n_full - 1)]
    chunks.append(((n_full - 1) * KV_TILE, KV_TILE + ROW_PAD, KV_TILE + N_META))

    def stacked_queries():
        return jnp.concatenate(
            [q_ref[:, g * HEAD_DIM:(g + 1) * HEAD_DIM] for g in range(ATTN_GROUP)], axis=0)

    def key_scores(q, idx):
        r0, rows, _ = chunks[idx]
        return lax.dot_general(k_ref[r0:r0 + rows, :], q, (((1,), (1,)), ((), ())),
                               preferred_element_type=F32)

    def mask_tail(idx, x, fill):
        _, rows, n_valid = chunks[idx]
        if n_valid is None:
            return x
        row = lax.broadcasted_iota(jnp.int32, (ROW_PAD, 1), 0) + (rows - ROW_PAD)
        tail = jnp.where(row < n_valid, x[rows - ROW_PAD:], fill)
        return jnp.concatenate([x[:rows - ROW_PAD], tail], axis=0)

    @pl.when((bounded_ref[0] != 0) & (t < n_tiles))
    def _():
        prev_acc, prev_l = acc_sc[...], l_sc[...]
        q = stacked_queries()
        acc = jnp.zeros(acc_sc.shape, F32)
        l = jnp.zeros(l_sc.shape, F32)
        for idx, (r0, rows, _) in enumerate(chunks):
            p = mask_tail(idx, jnp.exp2(key_scores(q, idx)), 0.0)
            l = l + jnp.sum(p, axis=0, keepdims=True)
            acc = acc + jnp.dot(vt_sc[:, r0:r0 + rows], p.astype(BF16),
                                preferred_element_type=F32)
        acc_sc[...] = acc
        l_sc[...] = l
        finish(prev_acc, prev_l)

    @pl.when((bounded_ref[0] == 0) & (t < n_tiles))
    def _():
        finish(acc_sc[...], l_sc[...])
        q = stacked_queries()
        m_sc[...] = jnp.full_like(m_sc, -jnp.inf)
        l_sc[...] = jnp.zeros_like(l_sc)
        acc_sc[...] = jnp.zeros_like(acc_sc)

        def scores(idx):
            rows = chunks[idx][1]
            s = mask_tail(idx, key_scores(q, idx), -jnp.inf)
            s_bufs[idx % 2][0:rows, :] = s
            cm_bufs[idx % 2][...] = jnp.max(s, axis=0, keepdims=True)

        def accumulate(idx):
            r0, rows, _ = chunks[idx]
            m_prev = m_sc[...]
            m_new = jnp.maximum(m_prev, cm_bufs[idx % 2][...])
            alpha = jnp.exp2(m_prev - m_new)
            p = jnp.exp2(s_bufs[idx % 2][0:rows, :] - m_new)
            l_sc[...] = alpha * l_sc[...] + jnp.sum(p, axis=0, keepdims=True)
            acc_sc[...] = alpha * acc_sc[...] + jnp.dot(
                vt_sc[:, r0:r0 + rows], p.astype(BF16), preferred_element_type=F32)
            m_sc[...] = m_new

        scores(0)
        for c in range(len(chunks)):
            if c + 1 < len(chunks):
                scores(c + 1)
            accumulate(c)

    @pl.when(t == n_tiles)
    def _():
        finish(acc_sc[...], l_sc[...])


def _attention(proj, bounded, *, batch, lp):
    np_ = proj.shape[0]
    s_real = lp - ROW_PAD
    tq = _pick_tile(lp, (384, 128))
    nq = lp // tq
    n_tiles = batch * ATTN_KV_HEADS * nq
    gw = ATTN_GROUP * HEAD_DIM
    k_col0 = ATTN_WIDTH // HEAD_DIM
    v_col0 = (ATTN_WIDTH + ATTN_KV_WIDTH) // HEAD_DIM

    def tile(t):
        t = jnp.minimum(t, n_tiles - 1)
        return t // (ATTN_KV_HEADS * nq), (t // nq) % ATTN_KV_HEADS, t % nq

    def q_block(t):
        b, h, i = tile(t)
        return b * nq + i, h

    return pl.pallas_call(
        functools.partial(_attn_kernel, s_real, tq, nq, n_tiles),
        grid=(n_tiles + 1,),
        in_specs=[
            pl.BlockSpec(memory_space=pltpu.SMEM),
            pl.BlockSpec((tq, gw), q_block),
            pl.BlockSpec((lp, HEAD_DIM), lambda t: (tile(t)[0], k_col0 + tile(t)[1])),
            pl.BlockSpec((lp, HEAD_DIM), lambda t: (tile(t)[0], v_col0 + tile(t)[1])),
        ],
        out_specs=pl.BlockSpec((tq, gw), lambda t: q_block(jnp.maximum(t - 1, 0))),
        out_shape=jax.ShapeDtypeStruct((np_, ATTN_WIDTH), BF16),
        scratch_shapes=[
            pltpu.VMEM((HEAD_DIM, lp), BF16),
            pltpu.VMEM((1, ATTN_GROUP * tq), F32),
            pltpu.VMEM((1, ATTN_GROUP * tq), F32),
            pltpu.VMEM((HEAD_DIM, ATTN_GROUP * tq), F32),
            pltpu.VMEM((KV_TILE + ROW_PAD, ATTN_GROUP * tq), F32),
            pltpu.VMEM((KV_TILE + ROW_PAD, ATTN_GROUP * tq), F32),
            pltpu.VMEM((1, ATTN_GROUP * tq), F32),
            pltpu.VMEM((1, ATTN_GROUP * tq), F32),
        ],
        compiler_params=pltpu.CompilerParams(
            dimension_semantics=("arbitrary",), vmem_limit_bytes=VMEM_LIMIT),
        name="attention",
    )(bounded, proj, proj, proj)


def _outproj_kernel(has_gate, *refs):
    it = iter(refs)
    x_ref = next(it)
    g_ref = next(it) if has_gate else None
    w_ref, h_ref, pw_ref, out_ref = next(it), next(it), next(it), next(it)
    x = x_ref[...]
    if has_gate:
        g = g_ref[...].astype(F32)
        x = (x.astype(F32) * (g * jax.nn.sigmoid(g))).astype(BF16)
    y = jnp.dot(x, w_ref[...], preferred_element_type=F32)
    r = lax.rsqrt(jnp.mean(y * y, axis=-1, keepdims=True) + NORM_EPS)
    out_ref[...] = h_ref[...] + y * r * pw_ref[...]


def _outproj(x, w, layer, h, post_w, *, batch, lp, rows, gate_src=None, gate_col_block=None):
    kdim = x.shape[1]
    d = w.shape[2]
    tm = _pick_tile(rows, (512, 384, 128))
    has_gate = gate_src is not None
    in_specs = [pl.BlockSpec((None, tm, kdim), lambda b, i: (b, i, 0))]
    args = [x.reshape(batch, lp, kdim)]
    if has_gate:
        in_specs.append(pl.BlockSpec((None, tm, kdim), lambda b, i: (b, i, gate_col_block)))
        args.append(gate_src.reshape(batch, lp, gate_src.shape[1]))
    in_specs += [
        pl.BlockSpec((None, kdim, d), lambda b, i: (layer, 0, 0)),
        pl.BlockSpec((None, tm, d), lambda b, i: (b, i, 0)),
        pl.BlockSpec((1, d), lambda b, i: (0, 0)),
    ]
    args += [w, h.reshape(batch, lp, d), post_w.reshape(1, d)]
    out = pl.pallas_call(
        functools.partial(_outproj_kernel, has_gate),
        grid=(batch, rows // tm),
        in_specs=in_specs,
        out_specs=pl.BlockSpec((None, tm, d), lambda b, i: (b, i, 0)),
        out_shape=jax.ShapeDtypeStruct((batch, rows, d), F32),
        compiler_params=pltpu.CompilerParams(
            dimension_semantics=("parallel", "parallel"), vmem_limit_bytes=VMEM_LIMIT),
        name="outproj_gated" if has_gate else "outproj",
    )(*args)
    return out.reshape(batch * rows, d)


def _gla_kernel(backward, n_blocks, batch, *refs):
    it = iter(refs)
    q_ref, k_ref, v_ref, lr_ref, up_ref, bias_ref = (next(it) for _ in range(6))
    if backward:
        of_ref, gate_ref, onorm_ref = next(it), next(it), next(it)
    out_ref, st_sc = next(it), next(it)

    i = pl.program_id(0)

    @pl.when(i == 0)
    def _():
        st_sc[...] = jnp.zeros_like(st_sc)

    is_meta = (i == n_blocks - 1) if backward else (i == 0)
    n_valid = jnp.where(is_meta, N_META, GLA_BLOCK)

    C, R = GLA_CHUNK, GLA_BLOCK
    row = lax.broadcasted_iota(jnp.int32, (R, R), 0)
    col = lax.broadcasted_iota(jnp.int32, (R, R), 1)
    same_chunk = (row // C) == (col // C)
    if backward:
        cum_mat = (same_chunk & (col >= row)).astype(BF16)
        own_mask = same_chunk & (col > row)
        cross_mask = (row < C) & (col >= C)
        e0, e1 = 0, C
    else:
        cum_mat = (same_chunk & (col <= row)).astype(BF16)
        own_mask = same_chunk & (col <= row)
        cross_mask = (row >= C) & (col < C)
        e0, e1 = C - 1, R - 1
    gate_cols = slice(GLA_GATE_RANK, 2 * GLA_GATE_RANK) if backward else slice(0, GLA_GATE_RANK)

    rows1 = lax.broadcasted_iota(jnp.int32, (R, 1), 0)
    valid = rows1 < n_valid
    first = rows1 < C
    up = up_ref[...].astype(BF16)

    def kcols(hd):
        return slice(hd * GLA_HEAD_K, (hd + 1) * GLA_HEAD_K)

    def vcols(hd):
        return slice(hd * GLA_HEAD_V, (hd + 1) * GLA_HEAD_V)

    nbat = range(batch)
    zs = [jnp.dot(lr_ref[bi, :, gate_cols].astype(BF16), up, preferred_element_type=F32)
          + bias_ref[...] for bi in nbat]
    gs = []
    for z in zs:
        log_sig = jnp.minimum(z, 0.0) - jnp.log1p(jnp.exp(-jnp.abs(z)))
        gs.append(_split_hi_lo(jnp.where(valid, log_sig / GLA_GATE_NORMALIZER, 0.0)))
    bs = [jnp.dot(cum_mat, g_hi, preferred_element_type=F32)
          + jnp.dot(cum_mat, g_lo, preferred_element_type=F32) for g_hi, g_lo in gs]

    staged = []
    for bi, b in zip(nbat, bs):
        tot0, tot1 = b[e0:e0 + 1, :], b[e1:e1 + 1, :]
        tot_own = jnp.where(first, tot0, tot1)
        q = q_ref[bi].astype(F32)
        k = jnp.where(valid, k_ref[bi].astype(F32), 0.0)
        qd = q * jnp.exp(b)
        ki = (k * jnp.exp(-b)).astype(BF16)
        ke = k * jnp.exp(tot_own - b)
        if backward:
            q_st = jnp.concatenate([qd[:C] * jnp.exp(tot1), qd[C:]], axis=0)
            ke_st = jnp.concatenate([ke[:C], ke[C:] * jnp.exp(tot0)], axis=0)
        else:
            q_st = jnp.concatenate([qd[:C], qd[C:] * jnp.exp(tot0)], axis=0)
            ke_st = jnp.concatenate([ke[:C] * jnp.exp(tot1), ke[C:]], axis=0)
        staged.append((qd.astype(BF16), ki, ke.astype(BF16), q_st.astype(BF16),
                       ke_st.astype(BF16), jnp.exp(tot0 + tot1)))

    heads = [(bi, hd) for bi in nbat for hd in range(GLA_HEADS)]
    pair_scores = []
    for bi, hd in heads:
        qd, ki, ke = staged[bi][:3]
        keys = jnp.concatenate([ki[:, kcols(hd)], ke[:, kcols(hd)]], axis=0)
        pair_scores.append(lax.dot_general(qd[:, kcols(hd)], keys, (((1,), (1,)), ((), ())),
                                           preferred_element_type=F32))
    for (bi, hd), p in zip(heads, pair_scores):
        _, _, _, q_st, ke_st, decay = staged[bi]
        v = jnp.where(valid, v_ref[bi, :, vcols(hd)], jnp.zeros((), BF16))
        sc = (jnp.where(own_mask, p[:, :R], 0.0)
              + jnp.where(cross_mask, p[:, R:], 0.0)).astype(BF16)
        st = st_sc[bi, hd]
        o = (jnp.dot(sc, v, preferred_element_type=F32)
             + lax.dot_general(q_st[:, kcols(hd)], st.astype(BF16), (((1,), (1,)), ((), ())),
                               preferred_element_type=F32))
        st_sc[bi, hd] = st * decay[:, kcols(hd)] + lax.dot_general(
            v, ke_st[:, kcols(hd)], (((0,), (0,)), ((), ())), preferred_element_type=F32)
        if backward:
            tot = o + of_ref[bi, :, vcols(hd)].astype(F32)
            r = lax.rsqrt(jnp.mean(tot * tot, axis=-1, keepdims=True) + NORM_EPS)
            gt = gate_ref[bi, :, vcols(hd)].astype(F32)
            out_ref[bi, :, vcols(hd)] = (tot * r * onorm_ref[...]
                                         * (gt * jax.nn.sigmoid(gt))).astype(BF16)
        else:
            out_ref[bi, :, vcols(hd)] = o.astype(BF16)


def _gla_direction(proj, lr, up, bias, *, batch, lp, backward, o_fwd=None, o_norm=None):
    np_ = proj.shape[0]
    nb = lp // GLA_BLOCK
    kb = GLA_KEY_DIM

    if backward:
        def blk(i):
            return jnp.where(i == nb - 1, nb - 1, nb - 2 - i)
    else:
        def blk(i):
            return jnp.where(i == 0, nb - 1, i - 1)

    def rows(width, col_block):
        return pl.BlockSpec((batch, GLA_BLOCK, width), lambda i: (0, blk(i), col_block))

    proj3 = proj.reshape(batch, lp, proj.shape[1])
    in_specs = [
        rows(kb, 0),
        rows(kb, 1),
        rows(GLA_VALUE_DIM, 1),
        rows(LANES, 0),
        pl.BlockSpec((GLA_GATE_RANK, kb), lambda i: (0, 0)),
        pl.BlockSpec((1, kb), lambda i: (0, 0)),
    ]
    args = [proj3, proj3, proj3, lr.reshape(batch, lp, LANES), up, bias.reshape(1, kb)]
    if backward:
        in_specs += [
            rows(GLA_VALUE_DIM, 0),
            rows(GLA_VALUE_DIM, 2),
            pl.BlockSpec((1, GLA_HEAD_V), lambda i: (0, 0)),
        ]
        args += [o_fwd.reshape(batch, lp, GLA_VALUE_DIM), proj3, o_norm.reshape(1, GLA_HEAD_V)]
    out = pl.pallas_call(
        functools.partial(_gla_kernel, backward, nb, batch),
        grid=(nb,),
        in_specs=in_specs,
        out_specs=rows(GLA_VALUE_DIM, 0),
        out_shape=jax.ShapeDtypeStruct((batch, lp, GLA_VALUE_DIM), BF16),
        scratch_shapes=[pltpu.VMEM((batch, GLA_HEADS, GLA_HEAD_V, GLA_HEAD_K), F32)],
        compiler_params=pltpu.CompilerParams(
            dimension_semantics=("arbitrary",), vmem_limit_bytes=VMEM_LIMIT),
        name="gla_bwd" if backward else "gla_fwd",
    )(*args)
    return out.reshape(np_, GLA_VALUE_DIM)


def _wprep_kernel(perm_cols, x_ref, o_ref):
    if perm_cols:
        group = (lax.broadcasted_iota(jnp.int32, (1, HEAD_DIM), 1) * 4) // HEAD_DIM
        for hh in range(perm_cols // HEAD_DIM):
            cols = slice(hh * HEAD_DIM, (hh + 1) * HEAD_DIM)
            xh = x_ref[:, cols]
            up32 = pltpu.roll(xh, HEAD_DIM - HEAD_DIM // 4, 1)
            down32 = pltpu.roll(xh, HEAD_DIM // 4, 1)
            o_ref[:, cols] = jnp.where(group == 1, up32,
                                       jnp.where(group == 2, down32, xh)).astype(BF16)
    o_ref[:, perm_cols:] = x_ref[:, perm_cols:].astype(BF16)


def _prep_weights(w, *, perm_cols=0):
    layers, k, n = w.shape
    tr = WPREP_ROWS
    return pl.pallas_call(
        functools.partial(_wprep_kernel, perm_cols),
        grid=(layers, k // tr),
        in_specs=[pl.BlockSpec((None, tr, n), lambda l, r: (l, r, 0))],
        out_specs=pl.BlockSpec((None, tr, n), lambda l, r: (l, r, 0)),
        out_shape=jax.ShapeDtypeStruct((layers, k, n), BF16),
        compiler_params=pltpu.CompilerParams(
            dimension_semantics=("parallel", "parallel"), vmem_limit_bytes=VMEM_LIMIT),
        name="weight_prep",
    )(w)


def _wprep_gla_kernel(scaled_tiles, scale, x_ref, tail_ref, o_ref, otail_ref):
    r = pl.program_id(1)
    o_ref[...] = (x_ref[...] * jnp.where(r < scaled_tiles, scale, 1.0)).astype(BF16)
    tail = tail_ref[...].astype(BF16)
    pad = jnp.zeros((LANES - tail.shape[0], tail.shape[1]), BF16)
    otail_ref[...] = jnp.concatenate([tail, pad], axis=0)


def _prep_gla_weights(w_t):
    layers, n_in, d = w_t.shape
    tail = n_in - GLA_MAIN
    tr = WPREP_ROWS_T
    return pl.pallas_call(
        functools.partial(_wprep_gla_kernel, GLA_KEY_DIM // tr, GLA_HEAD_K ** -0.5),
        grid=(layers, GLA_MAIN // tr),
        in_specs=[pl.BlockSpec((None, tr, d), lambda l, r: (l, r, 0)),
                  pl.BlockSpec((None, tail, d), lambda l, r: (l, GLA_MAIN // tail, 0))],
        out_specs=[pl.BlockSpec((None, tr, d), lambda l, r: (l, r, 0)),
                   pl.BlockSpec((None, LANES, d), lambda l, r: (l, 0, 0))],
        out_shape=[jax.ShapeDtypeStruct((layers, GLA_MAIN, d), BF16),
                   jax.ShapeDtypeStruct((layers, LANES, d), BF16)],
        compiler_params=pltpu.CompilerParams(
            dimension_semantics=("parallel", "arbitrary"), vmem_limit_bytes=VMEM_LIMIT),
        name="weight_prep_gla",
    )(w_t, w_t)


def _rope_perm():
    quarter = HEAD_DIM // 4
    idx = jnp.arange(HEAD_DIM).reshape(2, 2, quarter)
    return idx.transpose(1, 0, 2).reshape(HEAD_DIM)


def _rope_tables(s):
    rows = s // GRID_W
    row = jnp.repeat(jnp.arange(rows), GRID_W).astype(F32)
    col = jnp.tile(jnp.arange(GRID_W), rows).astype(F32)
    quarter = HEAD_DIM // 4
    inv_freq = ROPE_THETA ** (-jnp.arange(0, 2 * quarter, 2, dtype=F32) / (2 * quarter))
    ang_row = row[:, None] * inv_freq[None]
    ang_col = col[:, None] * inv_freq[None]
    ang = jnp.concatenate([ang_row, ang_col, ang_row, ang_col], axis=-1)
    ang = jnp.concatenate([ang, jnp.zeros((ROW_PAD, HEAD_DIM), F32)], axis=0)
    cos, sin = jnp.cos(ang), jnp.sin(ang)
    first_half = jnp.arange(HEAD_DIM) < HEAD_DIM // 2
    return cos, jnp.where(first_half[None], -sin, sin)


def kernel(x, meta_tokens, pre_norm, post_norm, attn_w_in, attn_q_norm, attn_k_norm,
           attn_w_out, gla_w_in, gla_gk_up, gla_gk_bias, gla_o_norm, gla_w_out):
    batch, s, d = x.shape
    assert d == D_MODEL and s % GLA_BLOCK == 0 and s % KV_TILE == 0 and s % GRID_W == 0
    lp = s + ROW_PAD
    depth = pre_norm.shape[0]

    meta = jnp.broadcast_to(meta_tokens.astype(x.dtype)[None], (batch, N_META, d))
    zeros = jnp.zeros((batch, ROW_PAD - N_META, d), x.dtype)
    h = jnp.concatenate([x, meta, zeros], axis=1).reshape(batch * lp, d)
    tables = _rope_tables(s)

    perm = _rope_perm()
    attn_w = _prep_weights(attn_w_in, perm_cols=ATTN_WIDTH + ATTN_KV_WIDTH)
    gla_w, gla_w_lr = _prep_gla_weights(jnp.swapaxes(gla_w_in, 1, 2))
    attn_wo = _prep_weights(attn_w_out)
    gla_wo = _prep_weights(gla_w_out)

    for i in range(depth):
        j = i // 2
        rows = s if i == depth - 1 else lp
        if i % 2 == 0:
            head_w = jnp.concatenate([
                jnp.tile(attn_q_norm[j][perm] * (HEAD_DIM ** -0.5 * LOG2_E), ATTN_HEADS),
                jnp.tile(attn_k_norm[j][perm], ATTN_KV_HEADS)]).reshape(1, -1)
            proj = _inproj(h, pre_norm[i], attn_w, j, ATTN_IN, lp=lp, rope=(head_w,) + tables)
            q_gain = jnp.max(jnp.abs(head_w[0, :ATTN_WIDTH]))
            k_gain = jnp.max(jnp.abs(head_w[0, ATTN_WIDTH:]))
            bounded = (HEAD_DIM * q_gain * k_gain <= SCORE_BOUND).astype(jnp.int32).reshape(1)
            o = _attention(proj, bounded, batch=batch, lp=lp)
            h = _outproj(o, attn_wo, j, h, post_norm[i], batch=batch, lp=lp, rows=rows,
                         gate_src=proj, gate_col_block=(ATTN_IN - ATTN_WIDTH) // ATTN_WIDTH)
        else:
            proj, lr = _inproj(h, pre_norm[i], gla_w, j, GLA_MAIN, lp=lp, w_lr=gla_w_lr,
                               w_transposed=True)
            o_f = _gla_direction(proj, lr, gla_gk_up[j, 0], gla_gk_bias[j, 0],
                                 batch=batch, lp=lp, backward=False)
            gated = _gla_direction(proj, lr, gla_gk_up[j, 1], gla_gk_bias[j, 1],
                                   batch=batch, lp=lp, backward=True,
                                   o_fwd=o_f, o_norm=gla_o_norm[j])
            h = _outproj(gated, gla_wo, j, h, post_norm[i], batch=batch, lp=lp, rows=rows)
    return h.reshape(batch, s, d)
```

```python
import functools

import jax
import jax.numpy as jnp
from jax import lax
from jax.experimental import pallas as pl
from jax.experimental.pallas import tpu as pltpu

F32 = jnp.float32
BF16 = jnp.bfloat16

D_MODEL = 2048
N_META = 16
GRID_W = 64
NORM_EPS = 1e-6
ROPE_THETA = 10000.0
LOG2_E = 1.4426950408889634

HEAD_DIM = 128
ATTN_HEADS = 16
ATTN_KV_HEADS = 8
ATTN_GROUP = ATTN_HEADS // ATTN_KV_HEADS
ATTN_WIDTH = ATTN_HEADS * HEAD_DIM
ATTN_KV_WIDTH = ATTN_KV_HEADS * HEAD_DIM
ATTN_IN = 2 * ATTN_WIDTH + 2 * ATTN_KV_WIDTH

GLA_HEADS = 4
GLA_KEY_DIM = D_MODEL // 2
GLA_VALUE_DIM = D_MODEL
GLA_HEAD_K = GLA_KEY_DIM // GLA_HEADS
GLA_HEAD_V = GLA_VALUE_DIM // GLA_HEADS
GLA_GATE_RANK = 16
GLA_GATE_NORMALIZER = 16.0
GLA_CHUNK = 64
GLA_MAIN = 2 * GLA_KEY_DIM + 2 * GLA_VALUE_DIM
GLA_IN = GLA_MAIN + 2 * GLA_GATE_RANK

LANES = 128
ROW_PAD = 128
GLA_BLOCK = 128
VMEM_LIMIT = 56 * 1024 * 1024

PROJ_TN = 1024
NORM_CHUNK = 128
WPREP_ROWS = 256
WPREP_ROWS_T = 512
KV_TILE = 1024
SCORE_BOUND = 60.0


def _pick_tile(n, candidates):
    for c in candidates:
        if n % c == 0:
            return c
    raise ValueError(f"no tile in {candidates} divides {n}")


def _split_hi_lo(a):
    hi = a.astype(BF16)
    lo = (a - hi.astype(F32)).astype(BF16)
    return hi, lo


def _inproj_kernel(n_i, n_j, n_rope_tiles, has_lr, tm, tn, w_transposed, k_chunks,
                   chunks_per_batch, *refs):
    it = iter(refs)
    h_refs = [next(it) for _ in range(k_chunks)]
    tail_ref = next(it) if chunks_per_batch else None
    pw_ref, w_ref = next(it), next(it)
    wlr_ref = next(it) if has_lr else None
    if n_rope_tiles:
        hn_ref, c_ref, sn_ref = next(it), next(it), next(it)
    o_ref = next(it)
    lr_ref = next(it) if has_lr else None
    xn_sc, xn_next_sc, acc_sc = next(it), next(it), next(it)

    t = pl.program_id(0)
    u = t - n_j
    n_tiles = n_i * n_j

    def project(x, w):
        contract = (((1,), (1,)), ((), ())) if w_transposed else (((1,), (0,)), ((), ()))
        return lax.dot_general(x, w, contract, preferred_element_type=F32)

    def normalise_chunk():
        for c, h_ref in enumerate(h_refs):
            chunk = jnp.minimum((t % n_j) * k_chunks + c, tm // NORM_CHUNK - 1)
            start = pl.multiple_of(chunk * NORM_CHUNK, NORM_CHUNK)
            x = h_ref[...]
            if chunks_per_batch:
                row_tile = jnp.minimum(t // n_j, n_i - 1)
                global_chunk = row_tile * (tm // NORM_CHUNK) + chunk
                is_tail = global_chunk % chunks_per_batch == chunks_per_batch - 1
                x = jnp.where(is_tail, tail_ref[...], x)
            ms = jnp.mean(x * x, axis=-1, keepdims=True)
            xn = (x * lax.rsqrt(ms + NORM_EPS) * pw_ref[...]).astype(BF16)
            xn_next_sc[pl.ds(start, NORM_CHUNK), :] = xn
            if has_lr:
                lr_ref[pl.ds(start, NORM_CHUNK), :] = project(xn, wlr_ref[...])

    def finish(acc, rope_tile):
        if not rope_tile:
            o_ref[...] = acc.astype(BF16)
            return
        c, sn = c_ref[...], sn_ref[...]
        for hh in range(tn // HEAD_DIM):
            cols = slice(hh * HEAD_DIM, (hh + 1) * HEAD_DIM)
            a = acc[:, cols]
            r = lax.rsqrt(jnp.mean(a * a, axis=-1, keepdims=True) + NORM_EPS)
            y = a * r * hn_ref[:, cols]
            y = y * c + pltpu.roll(y, HEAD_DIM // 2, 1) * sn
            o_ref[:, cols] = y.astype(BF16)

    def body(rope_tile):
        prev = acc_sc[...]
        acc_sc[...] = project(xn_sc[...], w_ref[...])
        finish(prev, rope_tile)
        normalise_chunk()

    @pl.when(t == 0)
    def _():
        acc_sc[...] = jnp.zeros_like(acc_sc)

    @pl.when(u < 0)
    def _():
        normalise_chunk()

    in_body = (u >= 0) & (u < n_tiles)

    @pl.when(in_body & (u % n_j == 0))
    def _():
        xn_sc[...] = xn_next_sc[...]

    prev_is_rope = (jnp.maximum(u - 1, 0) % n_j) < n_rope_tiles
    if n_rope_tiles:
        @pl.when(in_body & prev_is_rope)
        def _():
            body(True)

    @pl.when(in_body & jnp.logical_not(prev_is_rope))
    def _():
        body(False)

    @pl.when(u == n_tiles)
    def _():
        finish(acc_sc[...], False)


def _inproj(h, pre_w, w, layer, n, *, batch, lp, tail=None, w_lr=None, rope=None,
            w_transposed=False):
    d = h.shape[1]
    np_ = batch * lp
    tm = _pick_tile(lp, (1408, 384, 128))
    tn = PROJ_TN
    n_i, n_j = np_ // tm, n // tn
    n_tiles = n_i * n_j
    chunks_per_tile = tm // NORM_CHUNK
    k_chunks = pl.cdiv(chunks_per_tile, n_j)
    tiles_per_batch = lp // tm
    has_lr = w_lr is not None
    n_rope_tiles = 0

    def tile(t):
        u = jnp.clip(t - n_j, 0, n_tiles - 1)
        return u // n_j, u % n_j

    def prev_tile(t):
        u = jnp.clip(t - n_j - 1, 0, n_tiles - 1)
        return u // n_j, u % n_j

    def norm_row(t):
        return jnp.minimum(t // n_j, n_i - 1)

    chunks_per_batch = lp // NORM_CHUNK if tail is not None else 0

    def h_chunk(c):
        def index(t):
            chunk = jnp.minimum((t % n_j) * k_chunks + c, chunks_per_tile - 1)
            g = norm_row(t) * chunks_per_tile + chunk
            if tail is None:
                return g, 0
            b, cb = g // chunks_per_batch, g % chunks_per_batch
            return b * (chunks_per_batch - 1) + jnp.minimum(cb, chunks_per_batch - 2), 0
        return pl.BlockSpec((NORM_CHUNK, d), index)

    in_specs = [h_chunk(c) for c in range(k_chunks)]
    if tail is not None:
        in_specs.append(pl.BlockSpec((NORM_CHUNK, d), lambda t: (0, 0)))
    in_specs += [
        pl.BlockSpec((1, d), lambda t: (0, 0)),
        (pl.BlockSpec((None, tn, d), lambda t: (layer, tile(t)[1], 0)) if w_transposed
         else pl.BlockSpec((None, d, tn), lambda t: (layer, 0, tile(t)[1]))),
    ]
    args = [h] * k_chunks + ([tail] if tail is not None else []) + [pre_w.reshape(1, d), w]
    if has_lr:
        in_specs.append(pl.BlockSpec((None, LANES, d), lambda t: (layer, 0, 0)) if w_transposed
                        else pl.BlockSpec((None, d, LANES), lambda t: (layer, 0, 0)))
        args.append(w_lr)
    if rope is not None:
        head_w, cos_t, sin_t = rope
        n_rope_tiles = head_w.shape[1] // tn
        last = n_rope_tiles - 1
        in_specs.append(pl.BlockSpec((1, tn), lambda t: (0, jnp.minimum(prev_tile(t)[1], last))))
        tab = pl.BlockSpec((tm, HEAD_DIM), lambda t: (prev_tile(t)[0] % tiles_per_batch, 0))
        in_specs += [tab, tab]
        args += [head_w, cos_t, sin_t]
    out_shape = [jax.ShapeDtypeStruct((np_, n), BF16)]
    out_specs = [pl.BlockSpec((tm, tn), lambda t: prev_tile(t))]
    if has_lr:
        out_shape.append(jax.ShapeDtypeStruct((np_, LANES), F32))
        out_specs.append(pl.BlockSpec((tm, LANES), lambda t: (norm_row(t), 0)))
    res = pl.pallas_call(
        functools.partial(_inproj_kernel, n_i, n_j, n_rope_tiles, has_lr, tm, tn, w_transposed,
                          k_chunks, chunks_per_batch),
        grid=(n_tiles + n_j + 1,),
        in_specs=in_specs,
        out_specs=out_specs,
        out_shape=out_shape,
        scratch_shapes=[pltpu.VMEM((tm, d), BF16), pltpu.VMEM((tm, d), BF16),
                        pltpu.VMEM((tm, tn), F32)],
        compiler_params=pltpu.CompilerParams(
            dimension_semantics=("arbitrary",), vmem_limit_bytes=VMEM_LIMIT),
        name="inproj_rope" if rope is not None else "inproj_gla",
    )(*args)
    return res if has_lr else res[0]


def _attn_kernel(s_real, tq, nq, n_tiles, bounded_ref, q_ref, k_ref, v_ref, gate_ref, o_ref,
                 vt_sc, m_sc, l_sc, acc_sc, s_a, s_b, cm_a, cm_b):
    lp = s_real + ROW_PAD
    s_bufs, cm_bufs = (s_a, s_b), (cm_a, cm_b)
    t = pl.program_id(0)

    @pl.when(t == 0)
    def _():
        acc_sc[...] = jnp.zeros_like(acc_sc)
        l_sc[...] = jnp.ones_like(l_sc)

    @pl.when((t % nq == 0) & (t < n_tiles))
    def _():
        for c in range(lp // LANES):
            cols = slice(c * LANES, (c + 1) * LANES)
            vt_sc[:, cols] = v_ref[cols, :].astype(F32).T.astype(BF16)

    def finish(acc, l):
        o = (acc / l).T
        for g in range(ATTN_GROUP):
            cols = slice(g * HEAD_DIM, (g + 1) * HEAD_DIM)
            gt = gate_ref[:, cols].astype(F32)
            o_ref[:, cols] = (o[g * tq:(g + 1) * tq] * (gt * jax.nn.sigmoid(gt))).astype(BF16)

    n_full = s_real // KV_TILE
    chunks = [(c * KV_TILE, KV_TILE, None) for c in range(n_full - 1)]
    chunks.append(((n_full - 1) * KV_TILE, KV_TILE + ROW_PAD, KV_TILE + N_META))

    def stacked_queries():
        return jnp.concatenate(
            [q_ref[:, g * HEAD_DIM:(g + 1) * HEAD_DIM] for g in range(ATTN_GROUP)], axis=0)

    def key_scores(q, idx):
        r0, rows, _ = chunks[idx]
        return lax.dot_general(k_ref[r0:r0 + rows, :], q, (((1,), (1,)), ((), ())),
                               preferred_element_type=F32)

    def mask_tail(idx, x, fill):
        _, rows, n_valid = chunks[idx]
        if n_valid is None:
            return x
        row = lax.broadcasted_iota(jnp.int32, (ROW_PAD, 1), 0) + (rows - ROW_PAD)
        tail = jnp.where(row < n_valid, x[rows - ROW_PAD:], fill)
        return jnp.concatenate([x[:rows - ROW_PAD], tail], axis=0)

    @pl.when((bounded_ref[0] != 0) & (t < n_tiles))
    def _():
        prev_acc, prev_l = acc_sc[...], l_sc[...]
        q = stacked_queries()
        acc = jnp.zeros(acc_sc.shape, F32)
        l = jnp.zeros(l_sc.shape, F32)
        for idx, (r0, rows, _) in enumerate(chunks):
            p = mask_tail(idx, jnp.exp2(key_scores(q, idx)), 0.0)
            l = l + jnp.sum(p, axis=0, keepdims=True)
            acc = acc + jnp.dot(vt_sc[:, r0:r0 + rows], p.astype(BF16),
                                preferred_element_type=F32)
        acc_sc[...] = acc
        l_sc[...] = l
        finish(prev_acc, prev_l)

    @pl.when((bounded_ref[0] == 0) & (t < n_tiles))
    def _():
        finish(acc_sc[...], l_sc[...])
        q = stacked_queries()
        m_sc[...] = jnp.full_like(m_sc, -jnp.inf)
        l_sc[...] = jnp.zeros_like(l_sc)
        acc_sc[...] = jnp.zeros_like(acc_sc)

        def scores(idx):
            rows = chunks[idx][1]
            s = mask_tail(idx, key_scores(q, idx), -jnp.inf)
            s_bufs[idx % 2][0:rows, :] = s
            cm_bufs[idx % 2][...] = jnp.max(s, axis=0, keepdims=True)

        def accumulate(idx):
            r0, rows, _ = chunks[idx]
            m_prev = m_sc[...]
            m_new = jnp.maximum(m_prev, cm_bufs[idx % 2][...])
            alpha = jnp.exp2(m_prev - m_new)
            p = jnp.exp2(s_bufs[idx % 2][0:rows, :] - m_new)
            l_sc[...] = alpha * l_sc[...] + jnp.sum(p, axis=0, keepdims=True)
            acc_sc[...] = alpha * acc_sc[...] + jnp.dot(
                vt_sc[:, r0:r0 + rows], p.astype(BF16), preferred_element_type=F32)
            m_sc[...] = m_new

        scores(0)
        for c in range(len(chunks)):
            if c + 1 < len(chunks):
                scores(c + 1)
            accumulate(c)

    @pl.when(t == n_tiles)
    def _():
        finish(acc_sc[...], l_sc[...])


def _attention(proj, bounded, *, batch, lp):
    np_ = proj.shape[0]
    s_real = lp - ROW_PAD
    tq = _pick_tile(lp, (384, 128))
    nq = lp // tq
    n_tiles = batch * ATTN_KV_HEADS * nq
    gw = ATTN_GROUP * HEAD_DIM
    k_col0 = ATTN_WIDTH // HEAD_DIM
    v_col0 = (ATTN_WIDTH + ATTN_KV_WIDTH) // HEAD_DIM

    def tile(t):
        t = jnp.minimum(t, n_tiles - 1)
        return t // (ATTN_KV_HEADS * nq), (t // nq) % ATTN_KV_HEADS, t % nq

    def q_block(t):
        b, h, i = tile(t)
        return b * nq + i, h

    def finished_block(t, col0=0):
        rows, h = q_block(jnp.maximum(t - 1, 0))
        return rows, col0 + h

    gate_col0 = (ATTN_IN - ATTN_WIDTH) // gw

    return pl.pallas_call(
        functools.partial(_attn_kernel, s_real, tq, nq, n_tiles),
        grid=(n_tiles + 1,),
        in_specs=[
            pl.BlockSpec(memory_space=pltpu.SMEM),
            pl.BlockSpec((tq, gw), q_block),
            pl.BlockSpec((lp, HEAD_DIM), lambda t: (tile(t)[0], k_col0 + tile(t)[1])),
            pl.BlockSpec((lp, HEAD_DIM), lambda t: (tile(t)[0], v_col0 + tile(t)[1])),
            pl.BlockSpec((tq, gw), lambda t: finished_block(t, gate_col0)),
        ],
        out_specs=pl.BlockSpec((tq, gw), finished_block),
        out_shape=jax.ShapeDtypeStruct((np_, ATTN_WIDTH), BF16),
        scratch_shapes=[
            pltpu.VMEM((HEAD_DIM, lp), BF16),
            pltpu.VMEM((1, ATTN_GROUP * tq), F32),
            pltpu.VMEM((1, ATTN_GROUP * tq), F32),
            pltpu.VMEM((HEAD_DIM, ATTN_GROUP * tq), F32),
            pltpu.VMEM((KV_TILE + ROW_PAD, ATTN_GROUP * tq), F32),
            pltpu.VMEM((KV_TILE + ROW_PAD, ATTN_GROUP * tq), F32),
            pltpu.VMEM((1, ATTN_GROUP * tq), F32),
            pltpu.VMEM((1, ATTN_GROUP * tq), F32),
        ],
        compiler_params=pltpu.CompilerParams(
            dimension_semantics=("arbitrary",), vmem_limit_bytes=VMEM_LIMIT),
        name="attention",
    )(bounded, proj, proj, proj, proj)


def _outproj_kernel(n_res, tail_chunk, x_ref, w_ref, *refs):
    res_refs, (pw_ref, out_ref) = refs[:n_res], refs[-2:]
    y = jnp.dot(x_ref[...], w_ref[...], preferred_element_type=F32)
    r = lax.rsqrt(jnp.mean(y * y, axis=-1, keepdims=True) + NORM_EPS)
    if tail_chunk is None:
        h = res_refs[0][...]
    else:
        tail_ref = refs[n_res]
        first = pl.program_id(1) * n_res
        h = jnp.concatenate([jnp.where(first + c == tail_chunk, tail_ref[...], ref[...])
                             for c, ref in enumerate(res_refs)], axis=0)
    out_ref[...] = h + y * r * pw_ref[...]


def _outproj(x, w, layer, h, post_w, *, batch, lp, rows, tail=None):
    kdim = x.shape[1]
    d = w.shape[2]
    tm = _pick_tile(rows, (512, 384, 128))
    in_specs = [
        pl.BlockSpec((None, tm, kdim), lambda b, i: (b, i, 0)),
        pl.BlockSpec((None, kdim, d), lambda b, i: (layer, 0, 0)),
    ]
    args = [x.reshape(batch, lp, kdim), w]
    if tail is None:
        n_res, tail_chunk = 1, None
        in_specs.append(pl.BlockSpec((None, tm, d), lambda b, i: (b, i, 0)))
        args.append(h.reshape(batch, lp, d))
    else:
        n_res, tail_chunk = tm // ROW_PAD, lp // ROW_PAD - 1
        tokens = h.reshape(batch, lp - ROW_PAD, d)
        for c in range(n_res):
            in_specs.append(pl.BlockSpec(
                (None, ROW_PAD, d),
                lambda b, i, c=c: (b, jnp.minimum(i * n_res + c, tail_chunk - 1), 0)))
            args.append(tokens)
        in_specs.append(pl.BlockSpec((ROW_PAD, d), lambda b, i: (0, 0)))
        args.append(tail)
    in_specs.append(pl.BlockSpec((1, d), lambda b, i: (0, 0)))
    args.append(post_w.reshape(1, d))
    out = pl.pallas_call(
        functools.partial(_outproj_kernel, n_res, tail_chunk),
        grid=(batch, rows // tm),
        in_specs=in_specs,
        out_specs=pl.BlockSpec((None, tm, d), lambda b, i: (b, i, 0)),
        out_shape=jax.ShapeDtypeStruct((batch, rows, d), F32),
        compiler_params=pltpu.CompilerParams(
            dimension_semantics=("parallel", "parallel"), vmem_limit_bytes=VMEM_LIMIT),
        name="outproj",
    )(*args)
    return out.reshape(batch * rows, d)


def _gla_kernel(backward, n_blocks, batch, *refs):
    it = iter(refs)
    q_ref, k_ref, v_ref, lr_ref, up_ref, bias_ref = (next(it) for _ in range(6))
    if backward:
        of_ref, gate_ref, onorm_ref = next(it), next(it), next(it)
    out_ref, st_sc = next(it), next(it)

    i = pl.program_id(0)

    @pl.when(i == 0)
    def _():
        st_sc[...] = jnp.zeros_like(st_sc)

    is_meta = (i == n_blocks - 1) if backward else (i == 0)
    n_valid = jnp.where(is_meta, N_META, GLA_BLOCK)

    C, R = GLA_CHUNK, GLA_BLOCK
    row = lax.broadcasted_iota(jnp.int32, (R, R), 0)
    col = lax.broadcasted_iota(jnp.int32, (R, R), 1)
    same_chunk = (row // C) == (col // C)
    if backward:
        cum_mat = (same_chunk & (col >= row)).astype(BF16)
        own_mask = same_chunk & (col > row)
        cross_mask = (row < C) & (col >= C)
        e0, e1 = 0, C
    else:
        cum_mat = (same_chunk & (col <= row)).astype(BF16)
        own_mask = same_chunk & (col <= row)
        cross_mask = (row >= C) & (col < C)
        e0, e1 = C - 1, R - 1
    gate_cols = slice(GLA_GATE_RANK, 2 * GLA_GATE_RANK) if backward else slice(0, GLA_GATE_RANK)

    rows1 = lax.broadcasted_iota(jnp.int32, (R, 1), 0)
    valid = rows1 < n_valid
    first = rows1 < C
    up = up_ref[...].astype(BF16)

    def kcols(hd):
        return slice(hd * GLA_HEAD_K, (hd + 1) * GLA_HEAD_K)

    def vcols(hd):
        return slice(hd * GLA_HEAD_V, (hd + 1) * GLA_HEAD_V)

    nbat = range(batch)
    zs = [jnp.dot(lr_ref[bi, :, gate_cols].astype(BF16), up, preferred_element_type=F32)
          + bias_ref[...] for bi in nbat]
    gs = []
    for z in zs:
        log_sig = jnp.minimum(z, 0.0) - jnp.log1p(jnp.exp(-jnp.abs(z)))
        gs.append(_split_hi_lo(jnp.where(valid, log_sig / GLA_GATE_NORMALIZER, 0.0)))
    bs = [jnp.dot(cum_mat, g_hi, preferred_element_type=F32)
          + jnp.dot(cum_mat, g_lo, preferred_element_type=F32) for g_hi, g_lo in gs]

    staged = []
    for bi, b in zip(nbat, bs):
        tot0, tot1 = b[e0:e0 + 1, :], b[e1:e1 + 1, :]
        tot_own = jnp.where(first, tot0, tot1)
        q = q_ref[bi].astype(F32)
        k = k_ref[bi].astype(F32)
        qd = q * jnp.exp(b)
        ki = (k * jnp.exp(-b)).astype(BF16)
        ke = k * jnp.exp(tot_own - b)
        if backward:
            q_st = jnp.concatenate([qd[:C] * jnp.exp(tot1), qd[C:]], axis=0)
            ke_st = jnp.concatenate([ke[:C], ke[C:] * jnp.exp(tot0)], axis=0)
        else:
            q_st = jnp.concatenate([qd[:C], qd[C:] * jnp.exp(tot0)], axis=0)
            ke_st = jnp.concatenate([ke[:C] * jnp.exp(tot1), ke[C:]], axis=0)
        staged.append((qd.astype(BF16), ki, ke.astype(BF16), q_st.astype(BF16),
                       ke_st.astype(BF16), jnp.exp(tot0 + tot1)))

    heads = [(bi, hd) for bi in nbat for hd in range(GLA_HEADS)]
    pair_scores = []
    for bi, hd in heads:
        qd, ki, ke = staged[bi][:3]
        keys = jnp.concatenate([ki[:, kcols(hd)], ke[:, kcols(hd)]], axis=0)
        pair_scores.append(lax.dot_general(qd[:, kcols(hd)], keys, (((1,), (1,)), ((), ())),
                                           preferred_element_type=F32))
    for (bi, hd), p in zip(heads, pair_scores):
        _, _, _, q_st, ke_st, decay = staged[bi]
        v = v_ref[bi, :, vcols(hd)]
        sc = (jnp.where(own_mask, p[:, :R], 0.0)
              + jnp.where(cross_mask, p[:, R:], 0.0)).astype(BF16)
        st = st_sc[bi, hd]
        o = (jnp.dot(sc, v, preferred_element_type=F32)
             + lax.dot_general(q_st[:, kcols(hd)], st.astype(BF16), (((1,), (1,)), ((), ())),
                               preferred_element_type=F32))
        st_sc[bi, hd] = st * decay[:, kcols(hd)] + lax.dot_general(
            v, ke_st[:, kcols(hd)], (((0,), (0,)), ((), ())), preferred_element_type=F32)
        if backward:
            tot = o + of_ref[bi, :, vcols(hd)].astype(F32)
            r = lax.rsqrt(jnp.mean(tot * tot, axis=-1, keepdims=True) + NORM_EPS)
            gt = gate_ref[bi, :, vcols(hd)].astype(F32)
            out_ref[bi, :, vcols(hd)] = (tot * r * onorm_ref[...]
                                         * (gt * jax.nn.sigmoid(gt))).astype(BF16)
        else:
            out_ref[bi, :, vcols(hd)] = o.astype(BF16)


def _gla_direction(proj, lr, up, bias, *, batch, lp, backward, o_fwd=None, o_norm=None):
    np_ = proj.shape[0]
    nb = lp // GLA_BLOCK
    kb = GLA_KEY_DIM

    if backward:
        def blk(i):
            return jnp.where(i == nb - 1, nb - 1, nb - 2 - i)
    else:
        def blk(i):
            return jnp.where(i == 0, nb - 1, i - 1)

    def rows(width, col_block):
        return pl.BlockSpec((batch, GLA_BLOCK, width), lambda i: (0, blk(i), col_block))

    proj3 = proj.reshape(batch, lp, proj.shape[1])
    in_specs = [
        rows(kb, 0),
        rows(kb, 1),
        rows(GLA_VALUE_DIM, 1),
        rows(LANES, 0),
        pl.BlockSpec((GLA_GATE_RANK, kb), lambda i: (0, 0)),
        pl.BlockSpec((1, kb), lambda i: (0, 0)),
    ]
    args = [proj3, proj3, proj3, lr.reshape(batch, lp, LANES), up, bias.reshape(1, kb)]
    if backward:
        in_specs += [
            rows(GLA_VALUE_DIM, 0),
            rows(GLA_VALUE_DIM, 2),
            pl.BlockSpec((1, GLA_HEAD_V), lambda i: (0, 0)),
        ]
        args += [o_fwd.reshape(batch, lp, GLA_VALUE_DIM), proj3, o_norm.reshape(1, GLA_HEAD_V)]
    out = pl.pallas_call(
        functools.partial(_gla_kernel, backward, nb, batch),
        grid=(nb,),
        in_specs=in_specs,
        out_specs=rows(GLA_VALUE_DIM, 0),
        out_shape=jax.ShapeDtypeStruct((batch, lp, GLA_VALUE_DIM), BF16),
        scratch_shapes=[pltpu.VMEM((batch, GLA_HEADS, GLA_HEAD_V, GLA_HEAD_K), F32)],
        compiler_params=pltpu.CompilerParams(
            dimension_semantics=("arbitrary",), vmem_limit_bytes=VMEM_LIMIT),
        name="gla_bwd" if backward else "gla_fwd",
    )(*args)
    return out.reshape(np_, GLA_VALUE_DIM)


def _wprep_kernel(perm_cols, x_ref, o_ref):
    if perm_cols:
        group = (lax.broadcasted_iota(jnp.int32, (1, HEAD_DIM), 1) * 4) // HEAD_DIM
        for hh in range(perm_cols // HEAD_DIM):
            cols = slice(hh * HEAD_DIM, (hh + 1) * HEAD_DIM)
            xh = x_ref[:, cols]
            up32 = pltpu.roll(xh, HEAD_DIM - HEAD_DIM // 4, 1)
            down32 = pltpu.roll(xh, HEAD_DIM // 4, 1)
            o_ref[:, cols] = jnp.where(group == 1, up32,
                                       jnp.where(group == 2, down32, xh)).astype(BF16)
    o_ref[:, perm_cols:] = x_ref[:, perm_cols:].astype(BF16)


def _prep_weights(w, *, perm_cols=0):
    layers, k, n = w.shape
    tr = WPREP_ROWS
    return pl.pallas_call(
        functools.partial(_wprep_kernel, perm_cols),
        grid=(layers, k // tr),
        in_specs=[pl.BlockSpec((None, tr, n), lambda l, r: (l, r, 0))],
        out_specs=pl.BlockSpec((None, tr, n), lambda l, r: (l, r, 0)),
        out_shape=jax.ShapeDtypeStruct((layers, k, n), BF16),
        compiler_params=pltpu.CompilerParams(
            dimension_semantics=("parallel", "parallel"), vmem_limit_bytes=VMEM_LIMIT),
        name="weight_prep",
    )(w)


def _wprep_gla_kernel(scaled_tiles, scale, x_ref, tail_ref, o_ref, otail_ref):
    r = pl.program_id(1)
    o_ref[...] = (x_ref[...] * jnp.where(r < scaled_tiles, scale, 1.0)).astype(BF16)
    tail = tail_ref[...].astype(BF16)
    pad = jnp.zeros((LANES - tail.shape[0], tail.shape[1]), BF16)
    otail_ref[...] = jnp.concatenate([tail, pad], axis=0)


def _prep_gla_weights(w_t):
    layers, n_in, d = w_t.shape
    tail = n_in - GLA_MAIN
    tr = WPREP_ROWS_T
    return pl.pallas_call(
        functools.partial(_wprep_gla_kernel, GLA_KEY_DIM // tr, GLA_HEAD_K ** -0.5),
        grid=(layers, GLA_MAIN // tr),
        in_specs=[pl.BlockSpec((None, tr, d), lambda l, r: (l, r, 0)),
                  pl.BlockSpec((None, tail, d), lambda l, r: (l, GLA_MAIN // tail, 0))],
        out_specs=[pl.BlockSpec((None, tr, d), lambda l, r: (l, r, 0)),
                   pl.BlockSpec((None, LANES, d), lambda l, r: (l, 0, 0))],
        out_shape=[jax.ShapeDtypeStruct((layers, GLA_MAIN, d), BF16),
                   jax.ShapeDtypeStruct((layers, LANES, d), BF16)],
        compiler_params=pltpu.CompilerParams(
            dimension_semantics=("parallel", "arbitrary"), vmem_limit_bytes=VMEM_LIMIT),
        name="weight_prep_gla",
    )(w_t, w_t)


def _rope_perm():
    quarter = HEAD_DIM // 4
    idx = jnp.arange(HEAD_DIM).reshape(2, 2, quarter)
    return idx.transpose(1, 0, 2).reshape(HEAD_DIM)


def _rope_tables(s):
    rows = s // GRID_W
    row = jnp.repeat(jnp.arange(rows), GRID_W).astype(F32)
    col = jnp.tile(jnp.arange(GRID_W), rows).astype(F32)
    quarter = HEAD_DIM // 4
    inv_freq = ROPE_THETA ** (-jnp.arange(0, 2 * quarter, 2, dtype=F32) / (2 * quarter))
    ang_row = row[:, None] * inv_freq[None]
    ang_col = col[:, None] * inv_freq[None]
    ang = jnp.concatenate([ang_row, ang_col, ang_row, ang_col], axis=-1)
    ang = jnp.concatenate([ang, jnp.zeros((ROW_PAD, HEAD_DIM), F32)], axis=0)
    cos, sin = jnp.cos(ang), jnp.sin(ang)
    first_half = jnp.arange(HEAD_DIM) < HEAD_DIM // 2
    return cos, jnp.where(first_half[None], -sin, sin)


def kernel(x, meta_tokens, pre_norm, post_norm, attn_w_in, attn_q_norm, attn_k_norm,
           attn_w_out, gla_w_in, gla_gk_up, gla_gk_bias, gla_o_norm, gla_w_out):
    batch, s, d = x.shape
    assert d == D_MODEL and s % GLA_BLOCK == 0 and s % KV_TILE == 0 and s % GRID_W == 0
    lp = s + ROW_PAD
    depth = pre_norm.shape[0]

    h = x.reshape(batch * s, d)
    tail = jnp.concatenate([meta_tokens.astype(x.dtype),
                            jnp.zeros((ROW_PAD - N_META, d), x.dtype)], axis=0)
    tables = _rope_tables(s)

    perm = _rope_perm()
    attn_w = _prep_weights(attn_w_in, perm_cols=ATTN_WIDTH + ATTN_KV_WIDTH)
    gla_w, gla_w_lr = _prep_gla_weights(jnp.swapaxes(gla_w_in, 1, 2))
    attn_wo = _prep_weights(attn_w_out)
    gla_wo = _prep_weights(gla_w_out)

    for i in range(depth):
        j = i // 2
        rows = s if i == depth - 1 else lp
        if i % 2 == 0:
            head_w = jnp.concatenate([
                jnp.tile(attn_q_norm[j][perm] * (HEAD_DIM ** -0.5 * LOG2_E), ATTN_HEADS),
                jnp.tile(attn_k_norm[j][perm], ATTN_KV_HEADS)]).reshape(1, -1)
            proj = _inproj(h, pre_norm[i], attn_w, j, ATTN_IN, batch=batch, lp=lp, tail=tail,
                           rope=(head_w,) + tables)
            q_gain = jnp.max(jnp.abs(head_w[0, :ATTN_WIDTH]))
            k_gain = jnp.max(jnp.abs(head_w[0, ATTN_WIDTH:]))
            bounded = (HEAD_DIM * q_gain * k_gain <= SCORE_BOUND).astype(jnp.int32).reshape(1)
            o = _attention(proj, bounded, batch=batch, lp=lp)
            h = _outproj(o, attn_wo, j, h, post_norm[i], batch=batch, lp=lp, rows=rows, tail=tail)
        else:
            proj, lr = _inproj(h, pre_norm[i], gla_w, j, GLA_MAIN, batch=batch, lp=lp, tail=tail,
                               w_lr=gla_w_lr, w_transposed=True)
            o_f = _gla_direction(proj, lr, gla_gk_up[j, 0], gla_gk_bias[j, 0],
                                 batch=batch, lp=lp, backward=False)
            gated = _gla_direction(proj, lr, gla_gk_up[j, 1], gla_gk_bias[j, 1],
                                   batch=batch, lp=lp, backward=True,
                                   o_fwd=o_f, o_norm=gla_o_norm[j])
            h = _outproj(gated, gla_wo, j, h, post_norm[i], batch=batch, lp=lp, rows=rows, tail=tail)
        tail = None
    return h.reshape(batch, s, d)
```

```python
import functools

import jax
import jax.numpy as jnp
from jax import lax
from jax.experimental import pallas as pl
from jax.experimental.pallas import tpu as pltpu

F32 = jnp.float32
BF16 = jnp.bfloat16

D_MODEL = 2048
N_META = 16
GRID_W = 64
NORM_EPS = 1e-6
ROPE_THETA = 10000.0
LOG2_E = 1.4426950408889634

HEAD_DIM = 128
ATTN_HEADS = 16
ATTN_KV_HEADS = 8
ATTN_GROUP = ATTN_HEADS // ATTN_KV_HEADS
ATTN_WIDTH = ATTN_HEADS * HEAD_DIM
ATTN_KV_WIDTH = ATTN_KV_HEADS * HEAD_DIM
ATTN_IN = 2 * ATTN_WIDTH + 2 * ATTN_KV_WIDTH

GLA_HEADS = 4
GLA_KEY_DIM = D_MODEL // 2
GLA_VALUE_DIM = D_MODEL
GLA_HEAD_K = GLA_KEY_DIM // GLA_HEADS
GLA_HEAD_V = GLA_VALUE_DIM // GLA_HEADS
GLA_GATE_RANK = 16
GLA_GATE_NORMALIZER = 16.0
GLA_CHUNK = 64
GLA_MAIN = 2 * GLA_KEY_DIM + 2 * GLA_VALUE_DIM
GLA_IN = GLA_MAIN + 2 * GLA_GATE_RANK

LANES = 128
ROW_PAD = 128
GLA_BLOCK = 128
VMEM_LIMIT = 56 * 1024 * 1024

PROJ_TN = 1024
NORM_CHUNK = 128
WPREP_ROWS = 256
WPREP_ROWS_T = 512
KV_TILE = 1024
SCORE_BOUND = 60.0


def _pick_tile(n, candidates):
    for c in candidates:
        if n % c == 0:
            return c
    raise ValueError(f"no tile in {candidates} divides {n}")


def _split_hi_lo(a):
    hi = a.astype(BF16)
    lo = (a - hi.astype(F32)).astype(BF16)
    return hi, lo


def _inproj_kernel(n_i, n_j, n_rope_tiles, has_lr, tm, tn, w_transposed, k_chunks,
                   chunks_per_batch, *refs):
    it = iter(refs)
    h_refs = [next(it) for _ in range(k_chunks)]
    tail_ref = next(it) if chunks_per_batch else None
    pw_ref, w_ref = next(it), next(it)
    wlr_ref = next(it) if has_lr else None
    if n_rope_tiles:
        hn_ref, c_ref, sn_ref = next(it), next(it), next(it)
    o_ref = next(it)
    lr_ref = next(it) if has_lr else None
    xn_sc, xn_next_sc, acc_sc = next(it), next(it), next(it)

    t = pl.program_id(0)
    u = t - n_j
    n_tiles = n_i * n_j

    def project(x, w):
        contract = (((1,), (1,)), ((), ())) if w_transposed else (((1,), (0,)), ((), ()))
        return lax.dot_general(x, w, contract, preferred_element_type=F32)

    def normalise_chunk():
        for c, h_ref in enumerate(h_refs):
            chunk = jnp.minimum((t % n_j) * k_chunks + c, tm // NORM_CHUNK - 1)
            start = pl.multiple_of(chunk * NORM_CHUNK, NORM_CHUNK)
            x = h_ref[...]
            if chunks_per_batch:
                row_tile = jnp.minimum(t // n_j, n_i - 1)
                global_chunk = row_tile * (tm // NORM_CHUNK) + chunk
                is_tail = global_chunk % chunks_per_batch == chunks_per_batch - 1
                x = jnp.where(is_tail, tail_ref[...], x)
            ms = jnp.mean(x * x, axis=-1, keepdims=True)
            xn = (x * lax.rsqrt(ms + NORM_EPS) * pw_ref[...]).astype(BF16)
            xn_next_sc[pl.ds(start, NORM_CHUNK), :] = xn
            if has_lr:
                lr_ref[pl.ds(start, NORM_CHUNK), :] = project(xn, wlr_ref[...])

    def finish(acc, rope_tile):
        if not rope_tile:
            o_ref[...] = acc.astype(BF16)
            return
        c, sn = c_ref[...], sn_ref[...]
        for hh in range(tn // HEAD_DIM):
            cols = slice(hh * HEAD_DIM, (hh + 1) * HEAD_DIM)
            a = acc[:, cols]
            r = lax.rsqrt(jnp.mean(a * a, axis=-1, keepdims=True) + NORM_EPS)
            y = a * r * hn_ref[:, cols]
            y = y * c + pltpu.roll(y, HEAD_DIM // 2, 1) * sn
            o_ref[:, cols] = y.astype(BF16)

    def body(rope_tile):
        prev = acc_sc[...]
        acc_sc[...] = project(xn_sc[...], w_ref[...])
        finish(prev, rope_tile)
        normalise_chunk()

    @pl.when(t == 0)
    def _():
        acc_sc[...] = jnp.zeros_like(acc_sc)

    @pl.when(u < 0)
    def _():
        normalise_chunk()

    in_body = (u >= 0) & (u < n_tiles)

    @pl.when(in_body & (u % n_j == 0))
    def _():
        xn_sc[...] = xn_next_sc[...]

    prev_is_rope = (jnp.maximum(u - 1, 0) % n_j) < n_rope_tiles
    if n_rope_tiles:
        @pl.when(in_body & prev_is_rope)
        def _():
            body(True)

    @pl.when(in_body & jnp.logical_not(prev_is_rope))
    def _():
        body(False)

    @pl.when(u == n_tiles)
    def _():
        finish(acc_sc[...], False)


def _inproj(h, pre_w, w, layer, n, *, batch, lp, tail=None, w_lr=None, rope=None,
            w_transposed=False):
    d = h.shape[1]
    np_ = batch * lp
    tm = _pick_tile(lp, (1408, 384, 128))
    tn = PROJ_TN
    n_i, n_j = np_ // tm, n // tn
    n_tiles = n_i * n_j
    chunks_per_tile = tm // NORM_CHUNK
    k_chunks = pl.cdiv(chunks_per_tile, n_j)
    tiles_per_batch = lp // tm
    has_lr = w_lr is not None
    n_rope_tiles = 0

    def tile(t):
        u = jnp.clip(t - n_j, 0, n_tiles - 1)
        return u // n_j, u % n_j

    def prev_tile(t):
        u = jnp.clip(t - n_j - 1, 0, n_tiles - 1)
        return u // n_j, u % n_j

    def norm_row(t):
        return jnp.minimum(t // n_j, n_i - 1)

    chunks_per_batch = lp // NORM_CHUNK if tail is not None else 0

    def h_chunk(c):
        def index(t):
            chunk = jnp.minimum((t % n_j) * k_chunks + c, chunks_per_tile - 1)
            g = norm_row(t) * chunks_per_tile + chunk
            if tail is None:
                return g, 0
            b, cb = g // chunks_per_batch, g % chunks_per_batch
            return b * (chunks_per_batch - 1) + jnp.minimum(cb, chunks_per_batch - 2), 0
        return pl.BlockSpec((NORM_CHUNK, d), index)

    in_specs = [h_chunk(c) for c in range(k_chunks)]
    if tail is not None:
        in_specs.append(pl.BlockSpec((NORM_CHUNK, d), lambda t: (0, 0)))
    in_specs += [
        pl.BlockSpec((1, d), lambda t: (0, 0)),
        (pl.BlockSpec((None, tn, d), lambda t: (layer, tile(t)[1], 0)) if w_transposed
         else pl.BlockSpec((None, d, tn), lambda t: (layer, 0, tile(t)[1]))),
    ]
    args = [h] * k_chunks + ([tail] if tail is not None else []) + [pre_w.reshape(1, d), w]
    if has_lr:
        in_specs.append(pl.BlockSpec((None, LANES, d), lambda t: (layer, 0, 0)) if w_transposed
                        else pl.BlockSpec((None, d, LANES), lambda t: (layer, 0, 0)))
        args.append(w_lr)
    if rope is not None:
        head_w, cos_t, sin_t = rope
        n_rope_tiles = head_w.shape[1] // tn
        last = n_rope_tiles - 1
        in_specs.append(pl.BlockSpec((1, tn), lambda t: (0, jnp.minimum(prev_tile(t)[1], last))))
        tab = pl.BlockSpec((tm, HEAD_DIM), lambda t: (prev_tile(t)[0] % tiles_per_batch, 0))
        in_specs += [tab, tab]
        args += [head_w, cos_t, sin_t]
    out_shape = [jax.ShapeDtypeStruct((np_, n), BF16)]
    out_specs = [pl.BlockSpec((tm, tn), lambda t: prev_tile(t))]
    if has_lr:
        out_shape.append(jax.ShapeDtypeStruct((np_, LANES), F32))
        out_specs.append(pl.BlockSpec((tm, LANES), lambda t: (norm_row(t), 0)))
    res = pl.pallas_call(
        functools.partial(_inproj_kernel, n_i, n_j, n_rope_tiles, has_lr, tm, tn, w_transposed,
                          k_chunks, chunks_per_batch),
        grid=(n_tiles + n_j + 1,),
        in_specs=in_specs,
        out_specs=out_specs,
        out_shape=out_shape,
        scratch_shapes=[pltpu.VMEM((tm, d), BF16), pltpu.VMEM((tm, d), BF16),
                        pltpu.VMEM((tm, tn), F32)],
        compiler_params=pltpu.CompilerParams(
            dimension_semantics=("arbitrary",), vmem_limit_bytes=VMEM_LIMIT),
        name="inproj_rope" if rope is not None else "inproj_gla",
    )(*args)
    return res if has_lr else res[0]


def _attn_kernel(s_real, tq, nq, n_tiles, bounded_ref, q_ref, k_ref, v_ref, gate_ref, o_ref,
                 vt_sc, m_sc, l_sc, acc_sc, s_a, s_b, cm_a, cm_b):
    lp = s_real + ROW_PAD
    s_bufs, cm_bufs = (s_a, s_b), (cm_a, cm_b)
    t = pl.program_id(0)

    @pl.when(t == 0)
    def _():
        acc_sc[...] = jnp.zeros_like(acc_sc)
        l_sc[...] = jnp.ones_like(l_sc)

    @pl.when((t % nq == 0) & (t < n_tiles))
    def _():
        for c in range(lp // LANES):
            cols = slice(c * LANES, (c + 1) * LANES)
            vt_sc[:, cols] = v_ref[cols, :].astype(F32).T.astype(BF16)

    def finish(acc, l):
        o = (acc / l).T
        for g in range(ATTN_GROUP):
            cols = slice(g * HEAD_DIM, (g + 1) * HEAD_DIM)
            gt = gate_ref[:, cols].astype(F32)
            o_ref[:, cols] = (o[g * tq:(g + 1) * tq] * (gt * jax.nn.sigmoid(gt))).astype(BF16)

    n_full = s_real // KV_TILE
    chunks = [(c * KV_TILE, KV_TILE) for c in range(n_full - 1)]
    chunks.append(((n_full - 1) * KV_TILE, KV_TILE + N_META))

    def stacked_queries():
        return jnp.concatenate(
            [q_ref[:, g * HEAD_DIM:(g + 1) * HEAD_DIM] for g in range(ATTN_GROUP)], axis=0)

    def key_scores(q, idx):
        r0, rows = chunks[idx]
        return lax.dot_general(k_ref[r0:r0 + rows, :], q, (((1,), (1,)), ((), ())),
                               preferred_element_type=F32)

    @pl.when((bounded_ref[0] != 0) & (t < n_tiles))
    def _():
        prev_acc, prev_l = acc_sc[...], l_sc[...]
        q = stacked_queries()
        acc = jnp.zeros(acc_sc.shape, F32)
        l = jnp.zeros(l_sc.shape, F32)
        for idx, (r0, rows) in enumerate(chunks):
            p = jnp.exp2(key_scores(q, idx))
            l = l + jnp.sum(p, axis=0, keepdims=True)
            acc = acc + jnp.dot(vt_sc[:, r0:r0 + rows], p.astype(BF16),
                                preferred_element_type=F32)
        acc_sc[...] = acc
        l_sc[...] = l
        finish(prev_acc, prev_l)

    @pl.when((bounded_ref[0] == 0) & (t < n_tiles))
    def _():
        finish(acc_sc[...], l_sc[...])
        q = stacked_queries()
        m_sc[...] = jnp.full_like(m_sc, -jnp.inf)
        l_sc[...] = jnp.zeros_like(l_sc)
        acc_sc[...] = jnp.zeros_like(acc_sc)

        def scores(idx):
            rows = chunks[idx][1]
            s = key_scores(q, idx)
            s_bufs[idx % 2][0:rows, :] = s
            cm_bufs[idx % 2][...] = jnp.max(s, axis=0, keepdims=True)

        def accumulate(idx):
            r0, rows = chunks[idx]
            m_prev = m_sc[...]
            m_new = jnp.maximum(m_prev, cm_bufs[idx % 2][...])
            alpha = jnp.exp2(m_prev - m_new)
            p = jnp.exp2(s_bufs[idx % 2][0:rows, :] - m_new)
            l_sc[...] = alpha * l_sc[...] + jnp.sum(p, axis=0, keepdims=True)
            acc_sc[...] = alpha * acc_sc[...] + jnp.dot(
                vt_sc[:, r0:r0 + rows], p.astype(BF16), preferred_element_type=F32)
            m_sc[...] = m_new

        scores(0)
        for c in range(len(chunks)):
            if c + 1 < len(chunks):
                scores(c + 1)
            accumulate(c)

    @pl.when(t == n_tiles)
    def _():
        finish(acc_sc[...], l_sc[...])


def _attention(proj, bounded, *, batch, lp):
    np_ = proj.shape[0]
    s_real = lp - ROW_PAD
    tq = _pick_tile(lp, (384, 128))
    nq = lp // tq
    n_tiles = batch * ATTN_KV_HEADS * nq
    gw = ATTN_GROUP * HEAD_DIM
    k_col0 = ATTN_WIDTH // HEAD_DIM
    v_col0 = (ATTN_WIDTH + ATTN_KV_WIDTH) // HEAD_DIM

    def tile(t):
        t = jnp.minimum(t, n_tiles - 1)
        return t // (ATTN_KV_HEADS * nq), (t // nq) % ATTN_KV_HEADS, t % nq

    def q_block(t):
        b, h, i = tile(t)
        return b * nq + i, h

    def finished_block(t, col0=0):
        rows, h = q_block(jnp.maximum(t - 1, 0))
        return rows, col0 + h

    gate_col0 = (ATTN_IN - ATTN_WIDTH) // gw

    return pl.pallas_call(
        functools.partial(_attn_kernel, s_real, tq, nq, n_tiles),
        grid=(n_tiles + 1,),
        in_specs=[
            pl.BlockSpec(memory_space=pltpu.SMEM),
            pl.BlockSpec((tq, gw), q_block),
            pl.BlockSpec((lp, HEAD_DIM), lambda t: (tile(t)[0], k_col0 + tile(t)[1])),
            pl.BlockSpec((lp, HEAD_DIM), lambda t: (tile(t)[0], v_col0 + tile(t)[1])),
            pl.BlockSpec((tq, gw), lambda t: finished_block(t, gate_col0)),
        ],
        out_specs=pl.BlockSpec((tq, gw), finished_block),
        out_shape=jax.ShapeDtypeStruct((np_, ATTN_WIDTH), BF16),
        scratch_shapes=[
            pltpu.VMEM((HEAD_DIM, lp), BF16),
            pltpu.VMEM((1, ATTN_GROUP * tq), F32),
            pltpu.VMEM((1, ATTN_GROUP * tq), F32),
            pltpu.VMEM((HEAD_DIM, ATTN_GROUP * tq), F32),
            pltpu.VMEM((KV_TILE + N_META, ATTN_GROUP * tq), F32),
            pltpu.VMEM((KV_TILE + N_META, ATTN_GROUP * tq), F32),
            pltpu.VMEM((1, ATTN_GROUP * tq), F32),
            pltpu.VMEM((1, ATTN_GROUP * tq), F32),
        ],
        compiler_params=pltpu.CompilerParams(
            dimension_semantics=("arbitrary",), vmem_limit_bytes=VMEM_LIMIT),
        name="attention",
    )(bounded, proj, proj, proj, proj)


def _outproj_kernel(n_res, tail_chunk, x_ref, w_ref, *refs):
    res_refs, (pw_ref, out_ref) = refs[:n_res], refs[-2:]
    y = jnp.dot(x_ref[...], w_ref[...], preferred_element_type=F32)
    r = lax.rsqrt(jnp.mean(y * y, axis=-1, keepdims=True) + NORM_EPS)
    if tail_chunk is None:
        h = res_refs[0][...]
    else:
        tail_ref = refs[n_res]
        first = pl.program_id(1) * n_res
        h = jnp.concatenate([jnp.where(first + c == tail_chunk, tail_ref[...], ref[...])
                             for c, ref in enumerate(res_refs)], axis=0)
    out_ref[...] = h + y * r * pw_ref[...]


def _outproj(x, w, layer, h, post_w, *, batch, lp, rows, tail=None):
    kdim = x.shape[1]
    d = w.shape[2]
    tm = _pick_tile(rows, (512, 384, 128))
    in_specs = [
        pl.BlockSpec((None, tm, kdim), lambda b, i: (b, i, 0)),
        pl.BlockSpec((None, kdim, d), lambda b, i: (layer, 0, 0)),
    ]
    args = [x.reshape(batch, lp, kdim), w]
    if tail is None:
        n_res, tail_chunk = 1, None
        in_specs.append(pl.BlockSpec((None, tm, d), lambda b, i: (b, i, 0)))
        args.append(h.reshape(batch, lp, d))
    else:
        n_res, tail_chunk = tm // ROW_PAD, lp // ROW_PAD - 1
        tokens = h.reshape(batch, lp - ROW_PAD, d)
        for c in range(n_res):
            in_specs.append(pl.BlockSpec(
                (None, ROW_PAD, d),
                lambda b, i, c=c: (b, jnp.minimum(i * n_res + c, tail_chunk - 1), 0)))
            args.append(tokens)
        in_specs.append(pl.BlockSpec((ROW_PAD, d), lambda b, i: (0, 0)))
        args.append(tail)
    in_specs.append(pl.BlockSpec((1, d), lambda b, i: (0, 0)))
    args.append(post_w.reshape(1, d))
    out = pl.pallas_call(
        functools.partial(_outproj_kernel, n_res, tail_chunk),
        grid=(batch, rows // tm),
        in_specs=in_specs,
        out_specs=pl.BlockSpec((None, tm, d), lambda b, i: (b, i, 0)),
        out_shape=jax.ShapeDtypeStruct((batch, rows, d), F32),
        compiler_params=pltpu.CompilerParams(
            dimension_semantics=("parallel", "parallel"), vmem_limit_bytes=VMEM_LIMIT),
        name="outproj",
    )(*args)
    return out.reshape(batch * rows, d)


def _gla_kernel(backward, n_blocks, batch, *refs):
    it = iter(refs)
    q_ref, k_ref, v_ref, lr_ref, up_ref, bias_ref = (next(it) for _ in range(6))
    if backward:
        of_ref, gate_ref, onorm_ref = next(it), next(it), next(it)
    out_ref, st_sc = next(it), next(it)

    i = pl.program_id(0)

    @pl.when(i == 0)
    def _():
        st_sc[...] = jnp.zeros_like(st_sc)

    is_meta = (i == n_blocks - 1) if backward else (i == 0)
    n_valid = jnp.where(is_meta, N_META, GLA_BLOCK)

    C, R = GLA_CHUNK, GLA_BLOCK
    row = lax.broadcasted_iota(jnp.int32, (R, R), 0)
    col = lax.broadcasted_iota(jnp.int32, (R, R), 1)
    same_chunk = (row // C) == (col // C)
    if backward:
        cum_mat = (same_chunk & (col >= row)).astype(BF16)
        own_mask = same_chunk & (col > row)
        cross_mask = (row < C) & (col >= C)
        e0, e1 = 0, C
    else:
        cum_mat = (same_chunk & (col <= row)).astype(BF16)
        own_mask = same_chunk & (col <= row)
        cross_mask = (row >= C) & (col < C)
        e0, e1 = C - 1, R - 1
    gate_cols = slice(GLA_GATE_RANK, 2 * GLA_GATE_RANK) if backward else slice(0, GLA_GATE_RANK)

    rows1 = lax.broadcasted_iota(jnp.int32, (R, 1), 0)
    valid = rows1 < n_valid
    first = rows1 < C
    up = up_ref[...].astype(BF16)

    def kcols(hd):
        return slice(hd * GLA_HEAD_K, (hd + 1) * GLA_HEAD_K)

    def vcols(hd):
        return slice(hd * GLA_HEAD_V, (hd + 1) * GLA_HEAD_V)

    nbat = range(batch)
    zs = [jnp.dot(lr_ref[bi, :, gate_cols].astype(BF16), up, preferred_element_type=F32)
          + bias_ref[...] for bi in nbat]
    gs = []
    for z in zs:
        log_sig = jnp.minimum(z, 0.0) - jnp.log(1.0 + jnp.exp(-jnp.abs(z)))
        gs.append(_split_hi_lo(jnp.where(valid, log_sig / GLA_GATE_NORMALIZER, 0.0)))
    bs = [jnp.dot(cum_mat, g_hi, preferred_element_type=F32)
          + jnp.dot(cum_mat, g_lo, preferred_element_type=F32) for g_hi, g_lo in gs]

    staged = []
    for bi, b in zip(nbat, bs):
        tot0, tot1 = b[e0:e0 + 1, :], b[e1:e1 + 1, :]
        tot_own = jnp.where(first, tot0, tot1)
        q = q_ref[bi].astype(F32)
        k = k_ref[bi].astype(F32)
        qd = q * jnp.exp(b)
        ki = (k * jnp.exp(-b)).astype(BF16)
        ke = k * jnp.exp(tot_own - b)
        if backward:
            q_st = jnp.concatenate([qd[:C] * jnp.exp(tot1), qd[C:]], axis=0)
            ke_st = jnp.concatenate([ke[:C], ke[C:] * jnp.exp(tot0)], axis=0)
        else:
            q_st = jnp.concatenate([qd[:C], qd[C:] * jnp.exp(tot0)], axis=0)
            ke_st = jnp.concatenate([ke[:C] * jnp.exp(tot1), ke[C:]], axis=0)
        staged.append((qd.astype(BF16), ki, ke.astype(BF16), q_st.astype(BF16),
                       ke_st.astype(BF16), jnp.exp(tot0 + tot1)))

    heads = [(bi, hd) for bi in nbat for hd in range(GLA_HEADS)]
    pair_scores = []
    for bi, hd in heads:
        qd, ki, ke = staged[bi][:3]
        keys = jnp.concatenate([ki[:, kcols(hd)], ke[:, kcols(hd)]], axis=0)
        pair_scores.append(lax.dot_general(qd[:, kcols(hd)], keys, (((1,), (1,)), ((), ())),
                                           preferred_element_type=F32))
    for (bi, hd), p in zip(heads, pair_scores):
        _, _, _, q_st, ke_st, decay = staged[bi]
        v = v_ref[bi, :, vcols(hd)]
        sc = (jnp.where(own_mask, p[:, :R], 0.0)
              + jnp.where(cross_mask, p[:, R:], 0.0)).astype(BF16)
        st = st_sc[bi, hd]
        o = (jnp.dot(sc, v, preferred_element_type=F32)
             + lax.dot_general(q_st[:, kcols(hd)], st.astype(BF16), (((1,), (1,)), ((), ())),
                               preferred_element_type=F32))
        st_sc[bi, hd] = st * decay[:, kcols(hd)] + lax.dot_general(
            v, ke_st[:, kcols(hd)], (((0,), (0,)), ((), ())), preferred_element_type=F32)
        if backward:
            tot = o + of_ref[bi, :, vcols(hd)].astype(F32)
            r = lax.rsqrt(jnp.mean(tot * tot, axis=-1, keepdims=True) + NORM_EPS)
            gt = gate_ref[bi, :, vcols(hd)].astype(F32)
            out_ref[bi, :, vcols(hd)] = (tot * r * onorm_ref[...]
                                         * (gt * jax.nn.sigmoid(gt))).astype(BF16)
        else:
            out_ref[bi, :, vcols(hd)] = o.astype(BF16)


def _gla_direction(proj, lr, up, bias, *, batch, lp, backward, o_fwd=None, o_norm=None):
    np_ = proj.shape[0]
    nb = lp // GLA_BLOCK
    kb = GLA_KEY_DIM

    if backward:
        def blk(i):
            return jnp.where(i == nb - 1, nb - 1, nb - 2 - i)
    else:
        def blk(i):
            return jnp.where(i == 0, nb - 1, i - 1)

    def rows(width, col_block):
        return pl.BlockSpec((batch, GLA_BLOCK, width), lambda i: (0, blk(i), col_block))

    proj3 = proj.reshape(batch, lp, proj.shape[1])
    in_specs = [
        rows(kb, 0),
        rows(kb, 1),
        rows(GLA_VALUE_DIM, 1),
        rows(LANES, 0),
        pl.BlockSpec((GLA_GATE_RANK, kb), lambda i: (0, 0)),
        pl.BlockSpec((1, kb), lambda i: (0, 0)),
    ]
    args = [proj3, proj3, proj3, lr.reshape(batch, lp, LANES), up, bias.reshape(1, kb)]
    if backward:
        in_specs += [
            rows(GLA_VALUE_DIM, 0),
            rows(GLA_VALUE_DIM, 2),
            pl.BlockSpec((1, GLA_HEAD_V), lambda i: (0, 0)),
        ]
        args += [o_fwd.reshape(batch, lp, GLA_VALUE_DIM), proj3, o_norm.reshape(1, GLA_HEAD_V)]
    out = pl.pallas_call(
        functools.partial(_gla_kernel, backward, nb, batch),
        grid=(nb,),
        in_specs=in_specs,
        out_specs=rows(GLA_VALUE_DIM, 0),
        out_shape=jax.ShapeDtypeStruct((batch, lp, GLA_VALUE_DIM), BF16),
        scratch_shapes=[pltpu.VMEM((batch, GLA_HEADS, GLA_HEAD_V, GLA_HEAD_K), F32)],
        compiler_params=pltpu.CompilerParams(
            dimension_semantics=("arbitrary",), vmem_limit_bytes=VMEM_LIMIT),
        name="gla_bwd" if backward else "gla_fwd",
    )(*args)
    return out.reshape(np_, GLA_VALUE_DIM)


def _wprep_kernel(perm_cols, x_ref, o_ref):
    if perm_cols:
        group = (lax.broadcasted_iota(jnp.int32, (1, HEAD_DIM), 1) * 4) // HEAD_DIM
        for hh in range(perm_cols // HEAD_DIM):
            cols = slice(hh * HEAD_DIM, (hh + 1) * HEAD_DIM)
            xh = x_ref[:, cols]
            up32 = pltpu.roll(xh, HEAD_DIM - HEAD_DIM // 4, 1)
            down32 = pltpu.roll(xh, HEAD_DIM // 4, 1)
            o_ref[:, cols] = jnp.where(group == 1, up32,
                                       jnp.where(group == 2, down32, xh)).astype(BF16)
    o_ref[:, perm_cols:] = x_ref[:, perm_cols:].astype(BF16)


def _prep_weights(w, *, perm_cols=0):
    layers, k, n = w.shape
    tr = WPREP_ROWS
    return pl.pallas_call(
        functools.partial(_wprep_kernel, perm_cols),
        grid=(layers, k // tr),
        in_specs=[pl.BlockSpec((None, tr, n), lambda l, r: (l, r, 0))],
        out_specs=pl.BlockSpec((None, tr, n), lambda l, r: (l, r, 0)),
        out_shape=jax.ShapeDtypeStruct((layers, k, n), BF16),
        compiler_params=pltpu.CompilerParams(
            dimension_semantics=("parallel", "parallel"), vmem_limit_bytes=VMEM_LIMIT),
        name="weight_prep",
    )(w)


def _wprep_gla_kernel(scaled_tiles, scale, x_ref, tail_ref, o_ref, otail_ref):
    r = pl.program_id(1)
    o_ref[...] = (x_ref[...] * jnp.where(r < scaled_tiles, scale, 1.0)).astype(BF16)
    tail = tail_ref[...].astype(BF16)
    pad = jnp.zeros((LANES - tail.shape[0], tail.shape[1]), BF16)
    otail_ref[...] = jnp.concatenate([tail, pad], axis=0)


def _prep_gla_weights(w_t):
    layers, n_in, d = w_t.shape
    tail = n_in - GLA_MAIN
    tr = WPREP_ROWS_T
    return pl.pallas_call(
        functools.partial(_wprep_gla_kernel, GLA_KEY_DIM // tr, GLA_HEAD_K ** -0.5),
        grid=(layers, GLA_MAIN // tr),
        in_specs=[pl.BlockSpec((None, tr, d), lambda l, r: (l, r, 0)),
                  pl.BlockSpec((None, tail, d), lambda l, r: (l, GLA_MAIN // tail, 0))],
        out_specs=[pl.BlockSpec((None, tr, d), lambda l, r: (l, r, 0)),
                   pl.BlockSpec((None, LANES, d), lambda l, r: (l, 0, 0))],
        out_shape=[jax.ShapeDtypeStruct((layers, GLA_MAIN, d), BF16),
                   jax.ShapeDtypeStruct((layers, LANES, d), BF16)],
        compiler_params=pltpu.CompilerParams(
            dimension_semantics=("parallel", "arbitrary"), vmem_limit_bytes=VMEM_LIMIT),
        name="weight_prep_gla",
    )(w_t, w_t)


def _rope_perm():
    quarter = HEAD_DIM // 4
    idx = jnp.arange(HEAD_DIM).reshape(2, 2, quarter)
    return idx.transpose(1, 0, 2).reshape(HEAD_DIM)


def _rope_tables(s):
    rows = s // GRID_W
    row = jnp.repeat(jnp.arange(rows), GRID_W).astype(F32)
    col = jnp.tile(jnp.arange(GRID_W), rows).astype(F32)
    quarter = HEAD_DIM // 4
    inv_freq = ROPE_THETA ** (-jnp.arange(0, 2 * quarter, 2, dtype=F32) / (2 * quarter))
    ang_row = row[:, None] * inv_freq[None]
    ang_col = col[:, None] * inv_freq[None]
    ang = jnp.concatenate([ang_row, ang_col, ang_row, ang_col], axis=-1)
    ang = jnp.concatenate([ang, jnp.zeros((ROW_PAD, HEAD_DIM), F32)], axis=0)
    cos, sin = jnp.cos(ang), jnp.sin(ang)
    first_half = jnp.arange(HEAD_DIM) < HEAD_DIM // 2
    return cos, jnp.where(first_half[None], -sin, sin)


def kernel(x, meta_tokens, pre_norm, post_norm, attn_w_in, attn_q_norm, attn_k_norm,
           attn_w_out, gla_w_in, gla_gk_up, gla_gk_bias, gla_o_norm, gla_w_out):
    batch, s, d = x.shape
    assert d == D_MODEL and s % GLA_BLOCK == 0 and s % KV_TILE == 0 and s % GRID_W == 0
    lp = s + ROW_PAD
    depth = pre_norm.shape[0]

    h = x.reshape(batch * s, d)
    tail = jnp.concatenate([meta_tokens.astype(x.dtype),
                            jnp.zeros((ROW_PAD - N_META, d), x.dtype)], axis=0)
    tables = _rope_tables(s)

    perm = _rope_perm()
    attn_w = _prep_weights(attn_w_in, perm_cols=ATTN_WIDTH + ATTN_KV_WIDTH)
    gla_w, gla_w_lr = _prep_gla_weights(jnp.swapaxes(gla_w_in, 1, 2))
    attn_wo = _prep_weights(attn_w_out)
    gla_wo = _prep_weights(gla_w_out)

    for i in range(depth):
        j = i // 2
        rows = s if i == depth - 1 else lp
        if i % 2 == 0:
            head_w = jnp.concatenate([
                jnp.tile(attn_q_norm[j][perm] * (HEAD_DIM ** -0.5 * LOG2_E), ATTN_HEADS),
                jnp.tile(attn_k_norm[j][perm], ATTN_KV_HEADS)]).reshape(1, -1)
            proj = _inproj(h, pre_norm[i], attn_w, j, ATTN_IN, batch=batch, lp=lp, tail=tail,
                           rope=(head_w,) + tables)
            q_gain = jnp.max(jnp.abs(head_w[0, :ATTN_WIDTH]))
            k_gain = jnp.max(jnp.abs(head_w[0, ATTN_WIDTH:]))
            bounded = (HEAD_DIM * q_gain * k_gain <= SCORE_BOUND).astype(jnp.int32).reshape(1)
            o = _attention(proj, bounded, batch=batch, lp=lp)
            h = _outproj(o, attn_wo, j, h, post_norm[i], batch=batch, lp=lp, rows=rows, tail=tail)
        else:
            proj, lr = _inproj(h, pre_norm[i], gla_w, j, GLA_MAIN, batch=batch, lp=lp, tail=tail,
                               w_lr=gla_w_lr, w_transposed=True)
            o_f = _gla_direction(proj, lr, gla_gk_up[j, 0], gla_gk_bias[j, 0],
                                 batch=batch, lp=lp, backward=False)
            gated = _gla_direction(proj, lr, gla_gk_up[j, 1], gla_gk_bias[j, 1],
                                   batch=batch, lp=lp, backward=True,
                                   o_fwd=o_f, o_norm=gla_o_norm[j])
            h = _outproj(gated, gla_wo, j, h, post_norm[i], batch=batch, lp=lp, rows=rows, tail=tail)
        tail = None
    return h.reshape(batch, s, d)
```

```python
import functools

import jax
import jax.numpy as jnp
from jax import lax
from jax.experimental import pallas as pl
from jax.experimental.pallas import tpu as pltpu

F32 = jnp.float32
BF16 = jnp.bfloat16

D_MODEL = 2048
N_META = 16
GRID_W = 64
NORM_EPS = 1e-6
ROPE_THETA = 10000.0
LOG2_E = 1.4426950408889634

HEAD_DIM = 128
ATTN_HEADS = 16
ATTN_KV_HEADS = 8
ATTN_GROUP = ATTN_HEADS // ATTN_KV_HEADS
ATTN_WIDTH = ATTN_HEADS * HEAD_DIM
ATTN_KV_WIDTH = ATTN_KV_HEADS * HEAD_DIM
ATTN_IN = 2 * ATTN_WIDTH + 2 * ATTN_KV_WIDTH

GLA_HEADS = 4
GLA_KEY_DIM = D_MODEL // 2
GLA_VALUE_DIM = D_MODEL
GLA_HEAD_K = GLA_KEY_DIM // GLA_HEADS
GLA_HEAD_V = GLA_VALUE_DIM // GLA_HEADS
GLA_GATE_RANK = 16
GLA_GATE_NORMALIZER = 16.0
GLA_CHUNK = 64
GLA_MAIN = 2 * GLA_KEY_DIM + 2 * GLA_VALUE_DIM
GLA_IN = GLA_MAIN + 2 * GLA_GATE_RANK

LANES = 128
ROW_PAD = 128
GLA_BLOCK = 128
VMEM_LIMIT = 56 * 1024 * 1024

PROJ_TN = 1024
NORM_CHUNK = 128
WPREP_ROWS = 256
WPREP_ROWS_T = 512
KV_TILE = 1024
SCORE_BOUND = 60.0


def _pick_tile(n, candidates):
    for c in candidates:
        if n % c == 0:
            return c
    raise ValueError(f"no tile in {candidates} divides {n}")


def _inproj_kernel(n_i, n_j, n_rope_tiles, has_lr, tm, tn, w_transposed, k_chunks,
                   chunks_per_batch, *refs):
    it = iter(refs)
    h_refs = [next(it) for _ in range(k_chunks)]
    tail_ref = next(it) if chunks_per_batch else None
    pw_ref, w_ref = next(it), next(it)
    wlr_ref = next(it) if has_lr else None
    if n_rope_tiles:
        hn_ref, c_ref, sn_ref = next(it), next(it), next(it)
    o_ref = next(it)
    lr_ref = next(it) if has_lr else None
    xn_sc, xn_next_sc, acc_sc = next(it), next(it), next(it)

    t = pl.program_id(0)
    u = t - n_j
    n_tiles = n_i * n_j

    def project(x, w):
        contract = (((1,), (1,)), ((), ())) if w_transposed else (((1,), (0,)), ((), ()))
        return lax.dot_general(x, w, contract, preferred_element_type=F32)

    def normalise_chunk():
        for c, h_ref in enumerate(h_refs):
            chunk = jnp.minimum((t % n_j) * k_chunks + c, tm // NORM_CHUNK - 1)
            start = pl.multiple_of(chunk * NORM_CHUNK, NORM_CHUNK)
            x = h_ref[...]
            if chunks_per_batch:
                row_tile = jnp.minimum(t // n_j, n_i - 1)
                global_chunk = row_tile * (tm // NORM_CHUNK) + chunk
                is_tail = global_chunk % chunks_per_batch == chunks_per_batch - 1
                x = jnp.where(is_tail, tail_ref[...], x)
            ms = jnp.mean(x * x, axis=-1, keepdims=True)
            xn = (x * lax.rsqrt(ms + NORM_EPS) * pw_ref[...]).astype(BF16)
            xn_next_sc[pl.ds(start, NORM_CHUNK), :] = xn
            if has_lr:
                lr_ref[pl.ds(start, NORM_CHUNK), :] = project(xn, wlr_ref[...])

    def finish(acc, rope_tile):
        if not rope_tile:
            o_ref[...] = acc.astype(BF16)
            return
        c, sn = c_ref[...], sn_ref[...]
        for hh in range(tn // HEAD_DIM):
            cols = slice(hh * HEAD_DIM, (hh + 1) * HEAD_DIM)
            a = acc[:, cols]
            r = lax.rsqrt(jnp.mean(a * a, axis=-1, keepdims=True) + NORM_EPS)
            y = a * r * hn_ref[:, cols]
            y = y * c + pltpu.roll(y, HEAD_DIM // 2, 1) * sn
            o_ref[:, cols] = y.astype(BF16)

    def body(rope_tile):
        prev = acc_sc[...]
        acc_sc[...] = project(xn_sc[...], w_ref[...])
        finish(prev, rope_tile)
        normalise_chunk()

    @pl.when(t == 0)
    def _():
        acc_sc[...] = jnp.zeros_like(acc_sc)

    @pl.when(u < 0)
    def _():
        normalise_chunk()

    in_body = (u >= 0) & (u < n_tiles)

    @pl.when(in_body & (u % n_j == 0))
    def _():
        xn_sc[...] = xn_next_sc[...]

    prev_is_rope = (jnp.maximum(u - 1, 0) % n_j) < n_rope_tiles
    if n_rope_tiles:
        @pl.when(in_body & prev_is_rope)
        def _():
            body(True)

    @pl.when(in_body & jnp.logical_not(prev_is_rope))
    def _():
        body(False)

    @pl.when(u == n_tiles)
    def _():
        finish(acc_sc[...], False)


def _inproj(h, pre_w, w, layer, n, *, batch, lp, tail=None, w_lr=None, rope=None,
            w_transposed=False):
    d = h.shape[1]
    np_ = batch * lp
    tm = _pick_tile(lp, (1408, 384, 128))
    tn = PROJ_TN
    n_i, n_j = np_ // tm, n // tn
    n_tiles = n_i * n_j
    chunks_per_tile = tm // NORM_CHUNK
    k_chunks = pl.cdiv(chunks_per_tile, n_j)
    tiles_per_batch = lp // tm
    has_lr = w_lr is not None
    n_rope_tiles = 0

    def tile(t):
        u = jnp.clip(t - n_j, 0, n_tiles - 1)
        return u // n_j, u % n_j

    def prev_tile(t):
        u = jnp.clip(t - n_j - 1, 0, n_tiles - 1)
        return u // n_j, u % n_j

    def norm_row(t):
        return jnp.minimum(t // n_j, n_i - 1)

    chunks_per_batch = lp // NORM_CHUNK if tail is not None else 0

    def h_chunk(c):
        def index(t):
            chunk = jnp.minimum((t % n_j) * k_chunks + c, chunks_per_tile - 1)
            g = norm_row(t) * chunks_per_tile + chunk
            if tail is None:
                return g, 0
            b, cb = g // chunks_per_batch, g % chunks_per_batch
            return b * (chunks_per_batch - 1) + jnp.minimum(cb, chunks_per_batch - 2), 0
        return pl.BlockSpec((NORM_CHUNK, d), index)

    in_specs = [h_chunk(c) for c in range(k_chunks)]
    if tail is not None:
        in_specs.append(pl.BlockSpec((NORM_CHUNK, d), lambda t: (0, 0)))
    in_specs += [
        pl.BlockSpec((1, d), lambda t: (0, 0)),
        (pl.BlockSpec((None, tn, d), lambda t: (layer, tile(t)[1], 0)) if w_transposed
         else pl.BlockSpec((None, d, tn), lambda t: (layer, 0, tile(t)[1]))),
    ]
    args = [h] * k_chunks + ([tail] if tail is not None else []) + [pre_w.reshape(1, d), w]
    if has_lr:
        in_specs.append(pl.BlockSpec((None, LANES, d), lambda t: (layer, 0, 0)) if w_transposed
                        else pl.BlockSpec((None, d, LANES), lambda t: (layer, 0, 0)))
        args.append(w_lr)
    if rope is not None:
        head_w, cos_t, sin_t = rope
        n_rope_tiles = head_w.shape[1] // tn
        last = n_rope_tiles - 1
        in_specs.append(pl.BlockSpec((1, tn), lambda t: (0, jnp.minimum(prev_tile(t)[1], last))))
        tab = pl.BlockSpec((tm, HEAD_DIM), lambda t: (prev_tile(t)[0] % tiles_per_batch, 0))
        in_specs += [tab, tab]
        args += [head_w, cos_t, sin_t]
    out_shape = [jax.ShapeDtypeStruct((np_, n), BF16)]
    out_specs = [pl.BlockSpec((tm, tn), lambda t: prev_tile(t))]
    if has_lr:
        out_shape.append(jax.ShapeDtypeStruct((np_, LANES), F32))
        out_specs.append(pl.BlockSpec((tm, LANES), lambda t: (norm_row(t), 0)))
    res = pl.pallas_call(
        functools.partial(_inproj_kernel, n_i, n_j, n_rope_tiles, has_lr, tm, tn, w_transposed,
                          k_chunks, chunks_per_batch),
        grid=(n_tiles + n_j + 1,),
        in_specs=in_specs,
        out_specs=out_specs,
        out_shape=out_shape,
        scratch_shapes=[pltpu.VMEM((tm, d), BF16), pltpu.VMEM((tm, d), BF16),
                        pltpu.VMEM((tm, tn), F32)],
        compiler_params=pltpu.CompilerParams(
            dimension_semantics=("arbitrary",), vmem_limit_bytes=VMEM_LIMIT),
        name="inproj_rope" if rope is not None else "inproj_gla",
    )(*args)
    return res if has_lr else res[0]


def _attn_kernel(s_real, tq, nq, n_tiles, bounded_ref, q_ref, k_ref, v_ref, gate_ref, o_ref,
                 vt_sc, m_sc, l_sc, acc_sc, s_a, s_b, cm_a, cm_b):
    lp = s_real + ROW_PAD
    s_bufs, cm_bufs = (s_a, s_b), (cm_a, cm_b)
    t = pl.program_id(0)

    @pl.when(t == 0)
    def _():
        acc_sc[...] = jnp.zeros_like(acc_sc)
        l_sc[...] = jnp.ones_like(l_sc)

    @pl.when((t % nq == 0) & (t < n_tiles))
    def _():
        for c in range(lp // LANES):
            cols = slice(c * LANES, (c + 1) * LANES)
            vt_sc[:, cols] = v_ref[cols, :].astype(F32).T.astype(BF16)

    def finish(acc, l):
        o = (acc / l).T
        for g in range(ATTN_GROUP):
            cols = slice(g * HEAD_DIM, (g + 1) * HEAD_DIM)
            gt = gate_ref[:, cols].astype(F32)
            o_ref[:, cols] = (o[g * tq:(g + 1) * tq] * (gt * jax.nn.sigmoid(gt))).astype(BF16)

    n_full = s_real // KV_TILE
    chunks = [(c * KV_TILE, KV_TILE) for c in range(n_full - 1)]
    chunks.append(((n_full - 1) * KV_TILE, KV_TILE + N_META))

    def stacked_queries():
        return jnp.concatenate(
            [q_ref[:, g * HEAD_DIM:(g + 1) * HEAD_DIM] for g in range(ATTN_GROUP)], axis=0)

    def key_scores(q, idx):
        r0, rows = chunks[idx]
        return lax.dot_general(k_ref[r0:r0 + rows, :], q, (((1,), (1,)), ((), ())),
                               preferred_element_type=F32)

    @pl.when((bounded_ref[0] != 0) & (t < n_tiles))
    def _():
        prev_acc, prev_l = acc_sc[...], l_sc[...]
        q = stacked_queries()
        acc = jnp.zeros(acc_sc.shape, F32)
        l = jnp.zeros(l_sc.shape, F32)
        for idx, (r0, rows) in enumerate(chunks):
            p = jnp.exp2(key_scores(q, idx))
            l = l + jnp.sum(p, axis=0, keepdims=True)
            acc = acc + jnp.dot(vt_sc[:, r0:r0 + rows], p.astype(BF16),
                                preferred_element_type=F32)
        acc_sc[...] = acc
        l_sc[...] = l
        finish(prev_acc, prev_l)

    @pl.when((bounded_ref[0] == 0) & (t < n_tiles))
    def _():
        finish(acc_sc[...], l_sc[...])
        q = stacked_queries()
        m_sc[...] = jnp.full_like(m_sc, -jnp.inf)
        l_sc[...] = jnp.zeros_like(l_sc)
        acc_sc[...] = jnp.zeros_like(acc_sc)

        def scores(idx):
            rows = chunks[idx][1]
            s = key_scores(q, idx)
            s_bufs[idx % 2][0:rows, :] = s
            cm_bufs[idx % 2][...] = jnp.max(s, axis=0, keepdims=True)

        def accumulate(idx):
            r0, rows = chunks[idx]
            m_prev = m_sc[...]
            m_new = jnp.maximum(m_prev, cm_bufs[idx % 2][...])
            alpha = jnp.exp2(m_prev - m_new)
            p = jnp.exp2(s_bufs[idx % 2][0:rows, :] - m_new)
            l_sc[...] = alpha * l_sc[...] + jnp.sum(p, axis=0, keepdims=True)
            acc_sc[...] = alpha * acc_sc[...] + jnp.dot(
                vt_sc[:, r0:r0 + rows], p.astype(BF16), preferred_element_type=F32)
            m_sc[...] = m_new

        scores(0)
        for c in range(len(chunks)):
            if c + 1 < len(chunks):
                scores(c + 1)
            accumulate(c)

    @pl.when(t == n_tiles)
    def _():
        finish(acc_sc[...], l_sc[...])


def _attention(proj, bounded, *, batch, lp):
    np_ = proj.shape[0]
    s_real = lp - ROW_PAD
    tq = _pick_tile(lp, (384, 128))
    nq = lp // tq
    n_tiles = batch * ATTN_KV_HEADS * nq
    gw = ATTN_GROUP * HEAD_DIM
    k_col0 = ATTN_WIDTH // HEAD_DIM
    v_col0 = (ATTN_WIDTH + ATTN_KV_WIDTH) // HEAD_DIM

    def tile(t):
        t = jnp.minimum(t, n_tiles - 1)
        return t // (ATTN_KV_HEADS * nq), (t // nq) % ATTN_KV_HEADS, t % nq

    def q_block(t):
        b, h, i = tile(t)
        return b * nq + i, h

    def finished_block(t, col0=0):
        rows, h = q_block(jnp.maximum(t - 1, 0))
        return rows, col0 + h

    gate_col0 = (ATTN_IN - ATTN_WIDTH) // gw

    return pl.pallas_call(
        functools.partial(_attn_kernel, s_real, tq, nq, n_tiles),
        grid=(n_tiles + 1,),
        in_specs=[
            pl.BlockSpec(memory_space=pltpu.SMEM),
            pl.BlockSpec((tq, gw), q_block),
            pl.BlockSpec((lp, HEAD_DIM), lambda t: (tile(t)[0], k_col0 + tile(t)[1])),
            pl.BlockSpec((lp, HEAD_DIM), lambda t: (tile(t)[0], v_col0 + tile(t)[1])),
            pl.BlockSpec((tq, gw), lambda t: finished_block(t, gate_col0)),
        ],
        out_specs=pl.BlockSpec((tq, gw), finished_block),
        out_shape=jax.ShapeDtypeStruct((np_, ATTN_WIDTH), BF16),
        scratch_shapes=[
            pltpu.VMEM((HEAD_DIM, lp), BF16),
            pltpu.VMEM((1, ATTN_GROUP * tq), F32),
            pltpu.VMEM((1, ATTN_GROUP * tq), F32),
            pltpu.VMEM((HEAD_DIM, ATTN_GROUP * tq), F32),
            pltpu.VMEM((KV_TILE + N_META, ATTN_GROUP * tq), F32),
            pltpu.VMEM((KV_TILE + N_META, ATTN_GROUP * tq), F32),
            pltpu.VMEM((1, ATTN_GROUP * tq), F32),
            pltpu.VMEM((1, ATTN_GROUP * tq), F32),
        ],
        compiler_params=pltpu.CompilerParams(
            dimension_semantics=("arbitrary",), vmem_limit_bytes=VMEM_LIMIT),
        name="attention",
    )(bounded, proj, proj, proj, proj)


def _outproj_kernel(n_res, tail_chunk, x_ref, w_ref, *refs):
    res_refs, (pw_ref, out_ref) = refs[:n_res], refs[-2:]
    y = jnp.dot(x_ref[...], w_ref[...], preferred_element_type=F32)
    r = lax.rsqrt(jnp.mean(y * y, axis=-1, keepdims=True) + NORM_EPS)
    if tail_chunk is None:
        h = res_refs[0][...]
    else:
        tail_ref = refs[n_res]
        first = pl.program_id(1) * n_res
        h = jnp.concatenate([jnp.where(first + c == tail_chunk, tail_ref[...], ref[...])
                             for c, ref in enumerate(res_refs)], axis=0)
    out_ref[...] = h + y * r * pw_ref[...]


def _outproj(x, w, layer, h, post_w, *, batch, lp, rows, tail=None):
    kdim = x.shape[1]
    d = w.shape[2]
    tm = _pick_tile(rows, (512, 384, 128))
    in_specs = [
        pl.BlockSpec((None, tm, kdim), lambda b, i: (b, i, 0)),
        pl.BlockSpec((None, kdim, d), lambda b, i: (layer, 0, 0)),
    ]
    args = [x.reshape(batch, lp, kdim), w]
    if tail is None:
        n_res, tail_chunk = 1, None
        in_specs.append(pl.BlockSpec((None, tm, d), lambda b, i: (b, i, 0)))
        args.append(h.reshape(batch, lp, d))
    else:
        n_res, tail_chunk = tm // ROW_PAD, lp // ROW_PAD - 1
        tokens = h.reshape(batch, lp - ROW_PAD, d)
        for c in range(n_res):
            in_specs.append(pl.BlockSpec(
                (None, ROW_PAD, d),
                lambda b, i, c=c: (b, jnp.minimum(i * n_res + c, tail_chunk - 1), 0)))
            args.append(tokens)
        in_specs.append(pl.BlockSpec((ROW_PAD, d), lambda b, i: (0, 0)))
        args.append(tail)
    in_specs.append(pl.BlockSpec((1, d), lambda b, i: (0, 0)))
    args.append(post_w.reshape(1, d))
    out = pl.pallas_call(
        functools.partial(_outproj_kernel, n_res, tail_chunk),
        grid=(batch, rows // tm),
        in_specs=in_specs,
        out_specs=pl.BlockSpec((None, tm, d), lambda b, i: (b, i, 0)),
        out_shape=jax.ShapeDtypeStruct((batch, rows, d), F32),
        compiler_params=pltpu.CompilerParams(
            dimension_semantics=("parallel", "parallel"), vmem_limit_bytes=VMEM_LIMIT),
        name="outproj",
    )(*args)
    return out.reshape(batch * rows, d)


def _gla_kernel(backward, n_blocks, batch, *refs):
    it = iter(refs)
    q_ref, k_ref, v_ref, lr_ref, up_ref, bias_ref = (next(it) for _ in range(6))
    if backward:
        of_ref, gate_ref, onorm_ref = next(it), next(it), next(it)
    out_ref, st_sc = next(it), next(it)

    i = pl.program_id(0)

    @pl.when(i == 0)
    def _():
        st_sc[...] = jnp.zeros_like(st_sc)

    is_meta = (i == n_blocks - 1) if backward else (i == 0)
    n_valid = jnp.where(is_meta, N_META, GLA_BLOCK)

    C, R = GLA_CHUNK, GLA_BLOCK
    row = lax.broadcasted_iota(jnp.int32, (R, R), 0)
    col = lax.broadcasted_iota(jnp.int32, (R, R), 1)
    same_chunk = (row // C) == (col // C)
    if backward:
        cum_mat = (same_chunk & (col >= row)).astype(BF16)
        own_mask = same_chunk & (col > row)
        cross_mask = (row < C) & (col >= C)
        e0, e1 = 0, C
    else:
        cum_mat = (same_chunk & (col <= row)).astype(BF16)
        own_mask = same_chunk & (col <= row)
        cross_mask = (row >= C) & (col < C)
        e0, e1 = C - 1, R - 1
    gate_cols = slice(GLA_GATE_RANK, 2 * GLA_GATE_RANK) if backward else slice(0, GLA_GATE_RANK)

    rows1 = lax.broadcasted_iota(jnp.int32, (R, 1), 0)
    valid = rows1 < n_valid
    first = rows1 < C
    up = up_ref[...].astype(BF16)

    def kcols(hd):
        return slice(hd * GLA_HEAD_K, (hd + 1) * GLA_HEAD_K)

    def vcols(hd):
        return slice(hd * GLA_HEAD_V, (hd + 1) * GLA_HEAD_V)

    nbat = range(batch)
    zs = [jnp.dot(lr_ref[bi, :, gate_cols].astype(BF16), up, preferred_element_type=F32)
          + bias_ref[...] for bi in nbat]
    gs = []
    for z in zs:
        log_sig = jnp.minimum(z, 0.0) - jnp.log(1.0 + jnp.exp(-jnp.abs(z)))
        gs.append(jnp.where(valid, log_sig / GLA_GATE_NORMALIZER, 0.0).astype(BF16))
    bs = [jnp.dot(cum_mat, g, preferred_element_type=F32) for g in gs]

    staged = []
    for bi, b in zip(nbat, bs):
        tot0, tot1 = b[e0:e0 + 1, :], b[e1:e1 + 1, :]
        tot_own = jnp.where(first, tot0, tot1)
        q, k = q_ref[bi], k_ref[bi]
        qd = q * jnp.exp(b).astype(BF16)
        ki = k * jnp.exp(-b).astype(BF16)
        ke = k * jnp.exp(tot_own - b).astype(BF16)
        if backward:
            q_st = jnp.concatenate([qd[:C] * jnp.exp(tot1).astype(BF16), qd[C:]], axis=0)
            ke_st = jnp.concatenate([ke[:C], ke[C:] * jnp.exp(tot0).astype(BF16)], axis=0)
        else:
            q_st = jnp.concatenate([qd[:C], qd[C:] * jnp.exp(tot0).astype(BF16)], axis=0)
            ke_st = jnp.concatenate([ke[:C] * jnp.exp(tot1).astype(BF16), ke[C:]], axis=0)
        staged.append((qd, ki, ke, q_st, ke_st, jnp.exp(tot0 + tot1)))

    heads = [(bi, hd) for bi in nbat for hd in range(GLA_HEADS)]
    pair_scores = []
    for bi, hd in heads:
        qd, ki, ke = staged[bi][:3]
        keys = jnp.concatenate([ki[:, kcols(hd)], ke[:, kcols(hd)]], axis=0)
        pair_scores.append(lax.dot_general(qd[:, kcols(hd)], keys, (((1,), (1,)), ((), ())),
                                           preferred_element_type=F32))
    for (bi, hd), p in zip(heads, pair_scores):
        _, _, _, q_st, ke_st, decay = staged[bi]
        v = v_ref[bi, :, vcols(hd)]
        sc = (jnp.where(own_mask, p[:, :R], 0.0)
              + jnp.where(cross_mask, p[:, R:], 0.0)).astype(BF16)
        st = st_sc[bi, hd]
        o = (jnp.dot(sc, v, preferred_element_type=F32)
             + lax.dot_general(q_st[:, kcols(hd)], st.astype(BF16), (((1,), (1,)), ((), ())),
                               preferred_element_type=F32))
        st_sc[bi, hd] = st * decay[:, kcols(hd)] + lax.dot_general(
            v, ke_st[:, kcols(hd)], (((0,), (0,)), ((), ())), preferred_element_type=F32)
        if backward:
            tot = o + of_ref[bi, :, vcols(hd)].astype(F32)
            r = lax.rsqrt(jnp.mean(tot * tot, axis=-1, keepdims=True) + NORM_EPS)
            gt = gate_ref[bi, :, vcols(hd)].astype(F32)
            out_ref[bi, :, vcols(hd)] = (tot * r * onorm_ref[...]
                                         * (gt * jax.nn.sigmoid(gt))).astype(BF16)
        else:
            out_ref[bi, :, vcols(hd)] = o.astype(BF16)


def _gla_direction(proj, lr, up, bias, *, batch, lp, backward, o_fwd=None, o_norm=None):
    np_ = proj.shape[0]
    nb = lp // GLA_BLOCK
    kb = GLA_KEY_DIM

    if backward:
        def blk(i):
            return jnp.where(i == nb - 1, nb - 1, nb - 2 - i)
    else:
        def blk(i):
            return jnp.where(i == 0, nb - 1, i - 1)

    def rows(width, col_block):
        return pl.BlockSpec((batch, GLA_BLOCK, width), lambda i: (0, blk(i), col_block))

    proj3 = proj.reshape(batch, lp, proj.shape[1])
    in_specs = [
        rows(kb, 0),
        rows(kb, 1),
        rows(GLA_VALUE_DIM, 1),
        rows(LANES, 0),
        pl.BlockSpec((GLA_GATE_RANK, kb), lambda i: (0, 0)),
        pl.BlockSpec((1, kb), lambda i: (0, 0)),
    ]
    args = [proj3, proj3, proj3, lr.reshape(batch, lp, LANES), up, bias.reshape(1, kb)]
    if backward:
        in_specs += [
            rows(GLA_VALUE_DIM, 0),
            rows(GLA_VALUE_DIM, 2),
            pl.BlockSpec((1, GLA_HEAD_V), lambda i: (0, 0)),
        ]
        args += [o_fwd.reshape(batch, lp, GLA_VALUE_DIM), proj3, o_norm.reshape(1, GLA_HEAD_V)]
    out = pl.pallas_call(
        functools.partial(_gla_kernel, backward, nb, batch),
        grid=(nb,),
        in_specs=in_specs,
        out_specs=rows(GLA_VALUE_DIM, 0),
        out_shape=jax.ShapeDtypeStruct((batch, lp, GLA_VALUE_DIM), BF16),
        scratch_shapes=[pltpu.VMEM((batch, GLA_HEADS, GLA_HEAD_V, GLA_HEAD_K), F32)],
        compiler_params=pltpu.CompilerParams(
            dimension_semantics=("arbitrary",), vmem_limit_bytes=VMEM_LIMIT),
        name="gla_bwd" if backward else "gla_fwd",
    )(*args)
    return out.reshape(np_, GLA_VALUE_DIM)


def _wprep_kernel(perm_cols, x_ref, o_ref):
    if perm_cols:
        group = (lax.broadcasted_iota(jnp.int32, (1, HEAD_DIM), 1) * 4) // HEAD_DIM
        for hh in range(perm_cols // HEAD_DIM):
            cols = slice(hh * HEAD_DIM, (hh + 1) * HEAD_DIM)
            xh = x_ref[:, cols]
            up32 = pltpu.roll(xh, HEAD_DIM - HEAD_DIM // 4, 1)
            down32 = pltpu.roll(xh, HEAD_DIM // 4, 1)
            o_ref[:, cols] = jnp.where(group == 1, up32,
                                       jnp.where(group == 2, down32, xh)).astype(BF16)
    o_ref[:, perm_cols:] = x_ref[:, perm_cols:].astype(BF16)


def _prep_weights(w, *, perm_cols=0):
    layers, k, n = w.shape
    tr = WPREP_ROWS
    return pl.pallas_call(
        functools.partial(_wprep_kernel, perm_cols),
        grid=(layers, k // tr),
        in_specs=[pl.BlockSpec((None, tr, n), lambda l, r: (l, r, 0))],
        out_specs=pl.BlockSpec((None, tr, n), lambda l, r: (l, r, 0)),
        out_shape=jax.ShapeDtypeStruct((layers, k, n), BF16),
        compiler_params=pltpu.CompilerParams(
            dimension_semantics=("parallel", "parallel"), vmem_limit_bytes=VMEM_LIMIT),
        name="weight_prep",
    )(w)


def _wprep_gla_kernel(scaled_tiles, scale, x_ref, tail_ref, o_ref, otail_ref):
    r = pl.program_id(1)
    o_ref[...] = (x_ref[...] * jnp.where(r < scaled_tiles, scale, 1.0)).astype(BF16)
    tail = tail_ref[...].astype(BF16)
    pad = jnp.zeros((LANES - tail.shape[0], tail.shape[1]), BF16)
    otail_ref[...] = jnp.concatenate([tail, pad], axis=0)


def _prep_gla_weights(w_t):
    layers, n_in, d = w_t.shape
    tail = n_in - GLA_MAIN
    tr = WPREP_ROWS_T
    return pl.pallas_call(
        functools.partial(_wprep_gla_kernel, GLA_KEY_DIM // tr, GLA_HEAD_K ** -0.5),
        grid=(layers, GLA_MAIN // tr),
        in_specs=[pl.BlockSpec((None, tr, d), lambda l, r: (l, r, 0)),
                  pl.BlockSpec((None, tail, d), lambda l, r: (l, GLA_MAIN // tail, 0))],
        out_specs=[pl.BlockSpec((None, tr, d), lambda l, r: (l, r, 0)),
                   pl.BlockSpec((None, LANES, d), lambda l, r: (l, 0, 0))],
        out_shape=[jax.ShapeDtypeStruct((layers, GLA_MAIN, d), BF16),
                   jax.ShapeDtypeStruct((layers, LANES, d), BF16)],
        compiler_params=pltpu.CompilerParams(
            dimension_semantics=("parallel", "arbitrary"), vmem_limit_bytes=VMEM_LIMIT),
        name="weight_prep_gla",
    )(w_t, w_t)


def _rope_perm():
    quarter = HEAD_DIM // 4
    idx = jnp.arange(HEAD_DIM).reshape(2, 2, quarter)
    return idx.transpose(1, 0, 2).reshape(HEAD_DIM)


def _rope_tables(s):
    rows = s // GRID_W
    row = jnp.repeat(jnp.arange(rows), GRID_W).astype(F32)
    col = jnp.tile(jnp.arange(GRID_W), rows).astype(F32)
    quarter = HEAD_DIM // 4
    inv_freq = ROPE_THETA ** (-jnp.arange(0, 2 * quarter, 2, dtype=F32) / (2 * quarter))
    ang_row = row[:, None] * inv_freq[None]
    ang_col = col[:, None] * inv_freq[None]
    ang = jnp.concatenate([ang_row, ang_col, ang_row, ang_col], axis=-1)
    ang = jnp.concatenate([ang, jnp.zeros((ROW_PAD, HEAD_DIM), F32)], axis=0)
    cos, sin = jnp.cos(ang), jnp.sin(ang)
    first_half = jnp.arange(HEAD_DIM) < HEAD_DIM // 2
    return cos, jnp.where(first_half[None], -sin, sin)


def kernel(x, meta_tokens, pre_norm, post_norm, attn_w_in, attn_q_norm, attn_k_norm,
           attn_w_out, gla_w_in, gla_gk_up, gla_gk_bias, gla_o_norm, gla_w_out):
    batch, s, d = x.shape
    assert d == D_MODEL and s % GLA_BLOCK == 0 and s % KV_TILE == 0 and s % GRID_W == 0
    lp = s + ROW_PAD
    depth = pre_norm.shape[0]

    h = x.reshape(batch * s, d)
    tail = jnp.concatenate([meta_tokens.astype(x.dtype),
                            jnp.zeros((ROW_PAD - N_META, d), x.dtype)], axis=0)
    tables = _rope_tables(s)

    perm = _rope_perm()
    attn_w = _prep_weights(attn_w_in, perm_cols=ATTN_WIDTH + ATTN_KV_WIDTH)
    gla_w, gla_w_lr = _prep_gla_weights(jnp.swapaxes(gla_w_in, 1, 2))
    attn_wo = _prep_weights(attn_w_out)
    gla_wo = _prep_weights(gla_w_out)

    for i in range(depth):
        j = i // 2
        rows = s if i == depth - 1 else lp
        if i % 2 == 0:
            head_w = jnp.concatenate([
                jnp.tile(attn_q_norm[j][perm] * (HEAD_DIM ** -0.5 * LOG2_E), ATTN_HEADS),
                jnp.tile(attn_k_norm[j][perm], ATTN_KV_HEADS)]).reshape(1, -1)
            proj = _inproj(h, pre_norm[i], attn_w, j, ATTN_IN, batch=batch, lp=lp, tail=tail,
                           rope=(head_w,) + tables)
            q_gain = jnp.max(jnp.abs(head_w[0, :ATTN_WIDTH]))
            k_gain = jnp.max(jnp.abs(head_w[0, ATTN_WIDTH:]))
            bounded = (HEAD_DIM * q_gain * k_gain <= SCORE_BOUND).astype(jnp.int32).reshape(1)
            o = _attention(proj, bounded, batch=batch, lp=lp)
            h = _outproj(o, attn_wo, j, h, post_norm[i], batch=batch, lp=lp, rows=rows, tail=tail)
        else:
            proj, lr = _inproj(h, pre_norm[i], gla_w, j, GLA_MAIN, batch=batch, lp=lp, tail=tail,
                               w_lr=gla_w_lr, w_transposed=True)
            o_f = _gla_direction(proj, lr, gla_gk_up[j, 0], gla_gk_bias[j, 0],
                                 batch=batch, lp=lp, backward=False)
            gated = _gla_direction(proj, lr, gla_gk_up[j, 1], gla_gk_bias[j, 1],
                                   batch=batch, lp=lp, backward=True,
                                   o_fwd=o_f, o_norm=gla_o_norm[j])
            h = _outproj(gated, gla_wo, j, h, post_norm[i], batch=batch, lp=lp, rows=rows, tail=tail)
        tail = None
    return h.reshape(batch, s, d)
```
